```python
import math
import jax
import jax.numpy as jnp
from jax import lax
import numpy as np

D_MODEL = 2048
BATCH = 4
SEQ = 2048
DEPTH = 4
DEC_BATCH = 8
DEC_SEQ = 1
PAST_LEN = 16384
PAGE_SIZE = 128

HEAD_DIM = 128
H_MOBA = 6
H_DSA = 4
H_SB = 6
N_HEADS = H_MOBA + H_DSA + H_SB
MIX_W = N_HEADS * HEAD_DIM
ROPE_DIM = HEAD_DIM // 4
ROPE_THETA = 500000.0
MOBA_BLOCK = 256
MOBA_TOPK = 3
N_IDX_HEADS = 16
IDX_DIM = 64
IDX_ROPE_DIM = IDX_DIM // 4
DSA_TOPK = 256
N_BRANCH = 3
D_FF = 5632
Q_BLOCK = 128
N_MOD = 9
EPS = 1e-6
N_IN = 3 * MIX_W + N_IDX_HEADS * IDX_DIM + IDX_DIM + N_IDX_HEADS + N_BRANCH * D_MODEL

kernel_name = 'gated_hybrid_moba_dsa_stickbreak_step'


def rms_norm(x, g):
    xf = x.astype(jnp.float32)
    y = xf * lax.rsqrt(jnp.mean(xf * xf, axis=-1, keepdims=True) + EPS)
    return (y * g.astype(jnp.float32)).astype(x.dtype)


def swiglu(x, w1, w3, w2):
    return (jax.nn.silu(x @ w1) * (x @ w3)) @ w2


def rope(x, pos, rot_dim):
    half = rot_dim // 2
    inv_freq = jnp.power(ROPE_THETA, -jnp.arange(half, dtype=jnp.float32) * (2.0 / rot_dim))
    ang = pos.astype(jnp.float32)[:, None] * inv_freq[None, :]
    cos = jnp.cos(ang)[None, :, None, :]
    sin = jnp.sin(ang)[None, :, None, :]
    xf = x.astype(jnp.float32)
    x1 = xf[..., :half]
    x2 = xf[..., half:rot_dim]
    out = jnp.concatenate([x1 * cos - x2 * sin, x2 * cos + x1 * sin, xf[..., rot_dim:]], axis=-1)
    return out.astype(x.dtype)


def to_blocks(a):
    B, L = a.shape[0], a.shape[1]
    nb = -(-L // MOBA_BLOCK)
    a = jnp.pad(a, ((0, 0), (0, nb * MOBA_BLOCK - L), (0, 0), (0, 0)))
    return a.reshape(B, nb, MOBA_BLOCK, a.shape[2], a.shape[3])


def moba_attend(q, q_pos, k_blk, v_blk, k_mean):
    B, Tq, H, _ = q.shape
    nbk = k_blk.shape[1]
    own = q_pos // MOBA_BLOCK
    gate = jnp.einsum('bthd,bnhd->bthn', q, k_mean).astype(jnp.float32)
    fully_past = (jnp.arange(nbk)[None, :] < own[:, None])[None, :, None, :]
    gate = jnp.where(fully_past, gate, -jnp.inf)
    n_sel = min(MOBA_TOPK, nbk)
    _, sel = lax.top_k(gate, n_sel)
    blocks = [sel[..., j] for j in range(n_sel)] + [jnp.broadcast_to(own[None, :, None], (B, Tq, H))]
    b_ix = jnp.arange(B)[:, None, None]
    h_ix = jnp.arange(H)[None, None, :]
    offs = jnp.arange(MOBA_BLOCK)
    scores = []
    for j, blk in enumerate(blocks):
        kb = k_blk[b_ix, blk, :, h_ix]
        s = jnp.einsum('bthd,bthpd->bthp', q, kb).astype(jnp.float32) * (HEAD_DIM ** -0.5)
        if j < n_sel:
            ok = (j < own)[None, :, None, None]
        else:
            ok = ((own[:, None] * MOBA_BLOCK + offs[None, :]) <= q_pos[:, None])[None, :, None, :]
        scores.append(jnp.where(ok, s, -jnp.inf))
    p = jax.nn.softmax(jnp.concatenate(scores, axis=-1), axis=-1).astype(v_blk.dtype)
    out = None
    for j, blk in enumerate(blocks):
        vb = v_blk[b_ix, blk, :, h_ix]
        term = jnp.einsum('bthp,bthpd->bthd', p[..., j * MOBA_BLOCK:(j + 1) * MOBA_BLOCK], vb)
        out = term if out is None else out + term
    return out


def dsa_attend(q, qi, wi, q_pos, k_all, v_all, ki_all, n_keep):
    B, L = k_all.shape[0], k_all.shape[1]
    rel = jax.nn.relu(jnp.einsum('bthe,bse->bths', qi, ki_all).astype(jnp.float32) * (IDX_DIM ** -0.5))
    score = jnp.einsum('bth,bths->bts', wi.astype(jnp.float32), rel)
    admissible = jnp.arange(L)[None, None, :] <= q_pos[None, :, None]
    score = jnp.where(admissible, score, -jnp.inf)
    _, sel = lax.top_k(score, n_keep)
    b_ix = jnp.arange(B)[:, None, None]
    kg = k_all[b_ix, sel]
    s = jnp.einsum('bthd,btkhd->bthk', q, kg).astype(jnp.float32) * (HEAD_DIM ** -0.5)
    s = jnp.where((sel <= q_pos[None, :, None])[:, :, None, :], s, -jnp.inf)
    p = jax.nn.softmax(s, axis=-1).astype(v_all.dtype)
    return jnp.einsum('bthk,btkhd->bthd', p, v_all[b_ix, sel])


def stick_breaking_attend(q, q_pos, k_all, v_all):
    L = k_all.shape[1]
    z = jnp.einsum('bthd,bshd->bths', q, k_all).astype(jnp.float32) * (HEAD_DIM ** -0.5)
    strict = (jnp.arange(L)[None, :] < q_pos[:, None])[None, :, None, :]
    log_fail = jnp.where(strict, jax.nn.log_sigmoid(-z), 0.0)
    later = lax.cumsum(log_fail, axis=3, reverse=True) - log_fail
    w = jnp.where(strict, jnp.exp(jax.nn.log_sigmoid(z) + later), 0.0)
    return jnp.einsum('bths,bshd->bthd', w.astype(v_all.dtype), v_all)


def token_mixing(h, pos, n_keep, past, w_in, q_g, k_g, w_out):
    B, T, _ = h.shape
    proj = h @ w_in
    o1 = MIX_W
    o2 = 2 * MIX_W
    o3 = 3 * MIX_W
    o4 = o3 + N_IDX_HEADS * IDX_DIM
    o5 = o4 + IDX_DIM
    o6 = o5 + N_IDX_HEADS
    q = proj[..., :o1].reshape(B, T, N_HEADS, HEAD_DIM)
    k = proj[..., o1:o2].reshape(B, T, N_HEADS, HEAD_DIM)
    v = proj[..., o2:o3].reshape(B, T, N_HEADS, HEAD_DIM)
    qi = proj[..., o3:o4].reshape(B, T, N_IDX_HEADS, IDX_DIM)
    ki = proj[..., o4:o5]
    wi = proj[..., o5:o6] * (N_IDX_HEADS ** -0.5)
    gates = jax.nn.sigmoid(proj[..., o6:]).reshape(B, T, N_BRANCH, D_MODEL)
    a1 = H_MOBA
    a2 = H_MOBA + H_DSA
    q_a = rope(rms_norm(q[:, :, :a1], q_g[0]), pos, ROPE_DIM)
    k_a = rope(rms_norm(k[:, :, :a1], k_g[0]), pos, ROPE_DIM)
    q_b = rope(rms_norm(q[:, :, a1:a2], q_g[1]), pos, ROPE_DIM)
    k_b = rope(rms_norm(k[:, :, a1:a2], k_g[1]), pos, ROPE_DIM)
    q_c = q[:, :, a2:]
    k_c = k[:, :, a2:]
    qi = rope(qi, pos, IDX_ROPE_DIM)
    ki = rope(ki[:, :, None, :], pos, IDX_ROPE_DIM)[:, :, 0]
    k_new = jnp.concatenate([k_a, k_b, k_c], axis=2)
    if past is None:
        k_all, v_all, ki_all = k_new, v, ki
    else:
        k_all = jnp.concatenate([past[0], k_new], axis=1)
        v_all = jnp.concatenate([past[1], v], axis=1)
        ki_all = jnp.concatenate([past[2], ki], axis=1)
    k_blk = to_blocks(k_all[:, :, :a1])
    v_blk = to_blocks(v_all[:, :, :a1])
    k_mean = jnp.mean(k_blk.astype(jnp.float32), axis=2).astype(k_blk.dtype)
    k_b_all = k_all[:, :, a1:a2]
    v_b_all = v_all[:, :, a1:a2]
    k_c_all = k_all[:, :, a2:]
    v_c_all = v_all[:, :, a2:]
    qb = Q_BLOCK if T % Q_BLOCK == 0 else T

    def chunk(start):
        sl = lambda a: lax.dynamic_slice_in_dim(a, start, qb, axis=1)
        q_pos = lax.dynamic_slice_in_dim(pos, start, qb)
        return (moba_attend(sl(q_a), q_pos, k_blk, v_blk, k_mean),
                dsa_attend(sl(q_b), sl(qi), sl(wi), q_pos, k_b_all, v_b_all, ki_all, n_keep),
                stick_breaking_attend(sl(q_c), q_pos, k_c_all, v_c_all))

    outs_a, outs_b, outs_c = lax.map(chunk, jnp.arange(0, T, qb, dtype=jnp.int32))
    o_a = jnp.moveaxis(outs_a, 0, 1).reshape(B, T, H_MOBA * HEAD_DIM)
    o_b = jnp.moveaxis(outs_b, 0, 1).reshape(B, T, H_DSA * HEAD_DIM)
    o_c = jnp.moveaxis(outs_c, 0, 1).reshape(B, T, H_SB * HEAD_DIM)
    r1 = a1 * HEAD_DIM
    r2 = a2 * HEAD_DIM
    y = (gates[:, :, 0] * (o_a @ w_out[:r1])
         + gates[:, :, 1] * (o_b @ w_out[r1:r2])
         + gates[:, :, 2] * (o_c @ w_out[r2:]))
    return y, k_new, v, ki


def trunk_layer(x, c, pos, n_keep, past, w_ada, b_ada, norm_g, w_in, q_g, k_g, w_out, w1, w3, w2):
    B = x.shape[0]
    mod = (jax.nn.silu(c) @ w_ada + b_ada).reshape(B, N_MOD, 1, D_MODEL)

    def modulated(z, i, j):
        return rms_norm(z, norm_g[i]) * (1.0 + mod[:, j + 1]) + mod[:, j]

    x = x + 0.5 * mod[:, 2] * swiglu(modulated(x, 0, 0), w1[0], w3[0], w2[0])
    mix, k_new, v_new, ki_new = token_mixing(modulated(x, 1, 3), pos, n_keep, past, w_in, q_g, k_g, w_out)
    x = x + mod[:, 5] * mix
    x = x + 0.5 * mod[:, 8] * swiglu(modulated(x, 2, 6), w1[1], w3[1], w2[1])
    return x, k_new, v_new, ki_new


def setup_inputs(seed: int = 0) -> dict:
    key = jax.random.key(seed)
    ks = jax.random.split(key, 20)
    f32 = jnp.float32
    n_pages = PAST_LEN // PAGE_SIZE
    n_phys = (DEC_BATCH * n_pages * 5 + 3) // 4

    def nrm(k, shape, s):
        return jax.random.normal(k, shape, f32) * s

    page_table = jax.random.permutation(ks[0], n_phys)[: DEC_BATCH * n_pages].reshape(DEC_BATCH, n_pages).astype(jnp.int32)
    return {
        'x_prompt': nrm(ks[1], (BATCH, SEQ, D_MODEL), 1.0),
        'x_sample': nrm(ks[2], (DEC_BATCH, DEC_SEQ, D_MODEL), 1.0),
        'cache_k': nrm(ks[3], (n_phys, DEPTH, PAGE_SIZE, N_HEADS, HEAD_DIM), 1.0),
        'cache_v': nrm(ks[4], (n_phys, DEPTH, PAGE_SIZE, N_HEADS, HEAD_DIM), 1.0),
        'cache_kidx': nrm(ks[5], (n_phys, DEPTH, PAGE_SIZE, IDX_DIM), 1.0),
        'page_table': page_table,
        'c_prompt': nrm(ks[6], (BATCH, D_MODEL), 1.0),
        'c_sample': nrm(ks[7], (DEC_BATCH, D_MODEL), 1.0),
        'w_ada': nrm(ks[8], (DEPTH, D_MODEL, N_MOD * D_MODEL), D_MODEL ** -0.5),
        'b_ada': nrm(ks[9], (DEPTH, N_MOD * D_MODEL), 0.02),
        'norm_g': 1.0 + nrm(ks[10], (DEPTH, 3, D_MODEL), 0.02),
        'w_in': nrm(ks[11], (DEPTH, D_MODEL, N_IN), D_MODEL ** -0.5),
        'q_norm_g': 1.0 + nrm(ks[12], (DEPTH, 2, HEAD_DIM), 0.02),
        'k_norm_g': 1.0 + nrm(ks[13], (DEPTH, 2, HEAD_DIM), 0.02),
        'w_out': nrm(ks[14], (DEPTH, MIX_W, D_MODEL), MIX_W ** -0.5),
        'ffn_w1': nrm(ks[15], (DEPTH, 2, D_MODEL, D_FF), D_MODEL ** -0.5),
        'ffn_w3': nrm(ks[16], (DEPTH, 2, D_MODEL, D_FF), D_MODEL ** -0.5),
        'ffn_w2': nrm(ks[17], (DEPTH, 2, D_FF, D_MODEL), D_FF ** -0.5),
    }


def reference(x_prompt, x_sample, cache_k, cache_v, cache_kidx, page_table, c_prompt, c_sample,
              w_ada, b_ada, norm_g, w_in, q_norm_g, k_norm_g, w_out, ffn_w1, ffn_w3, ffn_w2):
    seq = x_prompt.shape[1]
    n_dec, dec_seq = x_sample.shape[0], x_sample.shape[1]
    past_len = page_table.shape[1] * PAGE_SIZE
    pos_p = jnp.arange(seq, dtype=jnp.int32)
    pos_s = past_len + jnp.arange(dec_seq, dtype=jnp.int32)
    keep_p = min(DSA_TOPK, seq // 4)
    keep_s = min(DSA_TOPK, (past_len + dec_seq) // 4)
    yp, ys = x_prompt, x_sample
    kp_l, vp_l, ip_l, ks_l, vs_l, is_l = [], [], [], [], [], []
    for l in range(DEPTH):
        lw = (w_ada[l], b_ada[l], norm_g[l], w_in[l], q_norm_g[l], k_norm_g[l], w_out[l],
              ffn_w1[l], ffn_w3[l], ffn_w2[l])
        yp, kp, vp, ip = trunk_layer(yp, c_prompt, pos_p, keep_p, None, *lw)
        past = (cache_k[page_table, l].reshape(n_dec, past_len, N_HEADS, HEAD_DIM),
                cache_v[page_table, l].reshape(n_dec, past_len, N_HEADS, HEAD_DIM),
                cache_kidx[page_table, l].reshape(n_dec, past_len, IDX_DIM))
        ys, k_s, v_s, i_s = trunk_layer(ys, c_sample, pos_s, keep_s, past, *lw)
        kp_l.append(kp)
        vp_l.append(vp)
        ip_l.append(ip)
        ks_l.append(k_s)
        vs_l.append(v_s)
        is_l.append(i_s)
    return (yp, ys, jnp.stack(kp_l, axis=1), jnp.stack(vp_l, axis=1), jnp.stack(ip_l, axis=1),
            jnp.stack(ks_l, axis=1), jnp.stack(vs_l, axis=1), jnp.stack(is_l, axis=1))
```

```python
import functools
import math

import jax
import jax.numpy as jnp
from jax import lax
from jax.experimental import pallas as pl
from jax.experimental.pallas import tpu as pltpu

HEAD_DIM = 128
H_MOBA = 6
H_DSA = 4
H_SB = 6
N_HEADS = H_MOBA + H_DSA + H_SB
MIX_W = N_HEADS * HEAD_DIM
ROPE_DIM = HEAD_DIM // 4
ROPE_THETA = 500000.0
MOBA_BLOCK = 256
MOBA_TOPK = 3
N_IDX_HEADS = 16
IDX_DIM = 64
IDX_ROPE_DIM = IDX_DIM // 4
IDX_W = N_IDX_HEADS * IDX_DIM
IDX_PAD_W = IDX_W + 128
DSA_TOPK = 256
N_BRANCH = 3
N_MOD = 9
EPS = 1e-6
LANES = 128
SAMPLE_ROWS = 16
VMEM_LIMIT = 56 * 1024 * 1024
NEG_INF = float("-inf")
INT_MIN = -(2 ** 31)
SCALE = HEAD_DIM ** -0.5

f32 = jnp.float32
bf16 = jnp.bfloat16


def _cparams(sem):
    return pltpu.CompilerParams(dimension_semantics=sem, vmem_limit_bytes=VMEM_LIMIT)


def _dot(a, b):
    return jnp.dot(a, b, preferred_element_type=f32)


def _dot_nt(a, b):
    return lax.dot_general(a, b, (((1,), (1,)), ((), ())), preferred_element_type=f32)


def _split_bf16(x):
    hi = x.astype(bf16)
    lo = (x - hi.astype(f32)).astype(bf16)
    return hi, lo


def _softplus(z):
    return jnp.maximum(z, 0.0) + jnp.log1p(jnp.exp(-jnp.abs(z)))


def _sortable_key(x):
    bits = pltpu.bitcast(x + 0.0, jnp.int32)
    return bits ^ ((bits >> 31) & jnp.int32(0x7FFFFFFF))


def _ada_kernel(c_ref, w_ref, b_ref, o_ref):
    c = c_ref[...]
    a = c * jax.nn.sigmoid(c)
    a_hi, a_lo = _split_bf16(a)
    w_hi, w_lo = _split_bf16(w_ref[...])
    o_ref[...] = _dot(a_hi, w_hi) + _dot(a_lo, w_hi) + _dot(a_hi, w_lo) + b_ref[...]


def _ada(c_all, w_ada, b_ada):
    depth, d, n = w_ada.shape
    rows = c_all.shape[0]
    tn = d // 2 if d >= 2 * LANES else d
    return pl.pallas_call(
        _ada_kernel,
        grid=(depth, n // tn),
        in_specs=[pl.BlockSpec((rows, d), lambda l, j: (0, 0)),
                  pl.BlockSpec((None, d, tn), lambda l, j: (l, 0, j)),
                  pl.BlockSpec((None, 1, tn), lambda l, j: (l, 0, j))],
        out_specs=pl.BlockSpec((None, rows, tn), lambda l, j: (l, 0, j)),
        out_shape=jax.ShapeDtypeStruct((depth, rows, n), f32),
        compiler_params=_cparams(("parallel", "parallel")),
        name="ada",
    )(c_all, w_ada, b_ada.reshape(depth, 1, n))


def _modnorm_kernel(x_ref, g_ref, shift_ref, scale_ref, o_ref):
    x = x_ref[...]
    ms = jnp.mean(x * x, axis=-1, keepdims=True)
    y = x * lax.rsqrt(ms + EPS) * g_ref[...]
    o_ref[...] = (y * (1.0 + scale_ref[...]) + shift_ref[...]).astype(o_ref.dtype)


def _mod_spec(mod, tm, rows_per_group):
    _, r, d = mod.shape
    return pl.BlockSpec((None, r, d), lambda i, *_: ((i * tm) // rows_per_group, 0, 0))


def _modnorm(x, g, shift, scale, tm, rows_per_group):
    m, d = x.shape
    return pl.pallas_call(
        _modnorm_kernel,
        grid=(m // tm,),
        in_specs=[pl.BlockSpec((tm, d), lambda i: (i, 0)),
                  pl.BlockSpec((1, d), lambda i: (0, 0)),
                  _mod_spec(shift, tm, rows_per_group),
                  _mod_spec(scale, tm, rows_per_group)],
        out_specs=pl.BlockSpec((tm, d), lambda i: (i, 0)),
        out_shape=jax.ShapeDtypeStruct((m, d), bf16),
        compiler_params=_cparams(("parallel",)),
        name="modnorm",
    )(x, g, shift, scale)


def _ffn_kernel(h_ref, x_ref, gate_ref, w1_ref, w3_ref, w2_ref, o_ref, acc_ref):
    f = pl.program_id(1)

    @pl.when(f == 0)
    def _():
        acc_ref[...] = jnp.zeros_like(acc_ref)

    h = h_ref[...]
    a = _dot(h, w1_ref[...])
    b = _dot(h, w3_ref[...])
    u = (a * jax.nn.sigmoid(a) * b).astype(bf16)
    acc_ref[...] += _dot(u, w2_ref[...])

    @pl.when(f == pl.num_programs(1) - 1)
    def _():
        o_ref[...] = x_ref[...] + 0.5 * gate_ref[...] * acc_ref[...]


def _ffn(h, x, gate, w1, w3, w2, tm, rows_per_group):
    m, d = x.shape
    dff = w1.shape[1]
    tf = min(512, dff)
    return pl.pallas_call(
        _ffn_kernel,
        grid=(m // tm, dff // tf),
        in_specs=[pl.BlockSpec((tm, d), lambda i, f: (i, 0)),
                  pl.BlockSpec((tm, d), lambda i, f: (i, 0)),
                  _mod_spec(gate, tm, rows_per_group),
                  pl.BlockSpec((d, tf), lambda i, f: (0, f)),
                  pl.BlockSpec((d, tf), lambda i, f: (0, f)),
                  pl.BlockSpec((tf, d), lambda i, f: (f, 0))],
        out_specs=pl.BlockSpec((tm, d), lambda i, f: (i, 0)),
        out_shape=jax.ShapeDtypeStruct((m, d), f32),
        scratch_shapes=[pltpu.VMEM((tm, d), f32)],
        compiler_params=_cparams(("parallel", "arbitrary")),
        name="ffn",
    )(h, x, gate, w1, w3, w2)


def _rope(x, c, a, b, shift):
    return x * c + pltpu.roll(x, LANES - shift, 1) * a + pltpu.roll(x, shift, 1) * b


def _proj_qk_kernel(h_ref, w_ref, gain_ref, flag_ref, c_ref, a_ref, b_ref, *o_refs):
    acc = _dot(h_ref[...], w_ref[...])
    c, a, b = c_ref[...], a_ref[...], b_ref[...]
    for hh in range(acc.shape[1] // HEAD_DIM):
        sl = slice(hh * HEAD_DIM, (hh + 1) * HEAD_DIM)
        x = acc[:, sl]
        ms = jnp.mean(x * x, axis=-1, keepdims=True)
        xn = x * lax.rsqrt(ms + EPS) * gain_ref[:, sl]
        out = jnp.where(flag_ref[:, sl] > 0.0, _rope(xn, c, a, b, ROPE_DIM // 2), x)
        for o_ref in o_refs:
            o_ref[:, sl] = out.astype(o_ref.dtype)


def _table_spec(table, tm):
    nblk = table.shape[0] // tm
    return pl.BlockSpec((tm, LANES), lambda i, *_: (i % nblk, 0))


def _proj_qk(h, w, gain, flag, tables, tm, want_f32):
    m, d = h.shape
    n = w.shape[1]
    tn = min(512, n)
    out_shape = [jax.ShapeDtypeStruct((m, n), bf16)]
    out_specs = [pl.BlockSpec((tm, tn), lambda i, j: (i, j))]
    if want_f32:
        out_shape.append(jax.ShapeDtypeStruct((m, n), f32))
        out_specs.append(pl.BlockSpec((tm, tn), lambda i, j: (i, j)))
    return pl.pallas_call(
        _proj_qk_kernel,
        grid=(m // tm, n // tn),
        in_specs=[pl.BlockSpec((tm, d), lambda i, j: (i, 0)),
                  pl.BlockSpec((d, tn), lambda i, j: (0, j)),
                  pl.BlockSpec((1, tn), lambda i, j: (0, j)),
                  pl.BlockSpec((1, tn), lambda i, j: (0, j))]
                 + [_table_spec(t, tm) for t in tables],
        out_specs=out_specs,
        out_shape=out_shape,
        compiler_params=_cparams(("parallel", "parallel")),
        name="proj_qk",
    )(h, w, gain, flag, *tables)


def _proj_v_kernel(h_ref, w_ref, o16_ref, o32_ref):
    acc = _dot(h_ref[...], w_ref[...])
    o16_ref[...] = acc.astype(bf16)
    o32_ref[...] = acc


def _proj_v(h, w, tm):
    m, d = h.shape
    n = w.shape[1]
    tn = min(512, n)
    return pl.pallas_call(
        _proj_v_kernel,
        grid=(m // tm, n // tn),
        in_specs=[pl.BlockSpec((tm, d), lambda i, j: (i, 0)),
                  pl.BlockSpec((d, tn), lambda i, j: (0, j))],
        out_specs=[pl.BlockSpec((tm, tn), lambda i, j: (i, j))] * 2,
        out_shape=[jax.ShapeDtypeStruct((m, n), bf16), jax.ShapeDtypeStruct((m, n), f32)],
        compiler_params=_cparams(("parallel", "parallel")),
        name="proj_v",
    )(h, w)


def _proj_idx_kernel(h_ref, w_ref, ci_ref, ai_ref, bi_ref, ck_ref, ak_ref, bk_ref, qi_ref, kw_ref, kk_ref):
    acc = _dot(h_ref[...], w_ref[...])
    ci, ai, bi = ci_ref[...], ai_ref[...], bi_ref[...]
    half = IDX_ROPE_DIM // 2
    for g in range(IDX_W // LANES):
        sl = slice(g * LANES, (g + 1) * LANES)
        qi_ref[:, sl] = _rope(acc[:, sl], ci, ai, bi, half).astype(bf16)
    chunk = _rope(acc[:, IDX_W:IDX_W + LANES], ck_ref[...], ak_ref[...], bk_ref[...], half)
    lane = lax.broadcasted_iota(jnp.int32, chunk.shape, 1)
    wi_scaled = chunk * (N_IDX_HEADS ** -0.5)
    kw_ref[...] = jnp.where(lane < IDX_DIM, chunk, jnp.where(lane < IDX_DIM + N_IDX_HEADS, wi_scaled, 0.0))
    kk_ref[...] = jnp.where(lane < IDX_DIM, chunk, pltpu.roll(chunk, IDX_DIM, 1)).astype(bf16)


def _proj_idx(h, w, tables, tm):
    m, d = h.shape
    return pl.pallas_call(
        _proj_idx_kernel,
        grid=(m // tm,),
        in_specs=[pl.BlockSpec((tm, d), lambda i: (i, 0)),
                  pl.BlockSpec((d, IDX_PAD_W), lambda i: (0, 0))]
                 + [_table_spec(t, tm) for t in tables],
        out_specs=[pl.BlockSpec((tm, IDX_W), lambda i: (i, 0)),
                   pl.BlockSpec((tm, LANES), lambda i: (i, 0)),
                   pl.BlockSpec((tm, LANES), lambda i: (i, 0))],
        out_shape=[jax.ShapeDtypeStruct((m, IDX_W), bf16),
                   jax.ShapeDtypeStruct((m, LANES), f32),
                   jax.ShapeDtypeStruct((m, LANES), bf16)],
        compiler_params=_cparams(("parallel",)),
        name="proj_idx",
    )(h, w, *tables)


_BRANCH_ROWS = ((0, H_MOBA * HEAD_DIM),
                (H_MOBA * HEAD_DIM, (H_MOBA + H_DSA) * HEAD_DIM),
                ((H_MOBA + H_DSA) * HEAD_DIM, MIX_W))


def _mixout_kernel(h_ref, o_ref, x_ref, gate_ref, wg0_ref, wg1_ref, wg2_ref, wo_ref, out_ref):
    h = h_ref[...]
    y = None
    for (r0, r1), wg_ref in zip(_BRANCH_ROWS, (wg0_ref, wg1_ref, wg2_ref)):
        g = jax.nn.sigmoid(_dot(h, wg_ref[...]))
        term = g * _dot(o_ref[:, r0:r1], wo_ref[r0:r1, :])
        y = term if y is None else y + term
    out_ref[...] = x_ref[...] + gate_ref[...] * y


def _mixout(h, o, x, gate, wg, wo, tm, rows_per_group):
    m, d = x.shape
    tn = min(512, d)

    def gate_spec(b):
        return pl.BlockSpec((None, d, tn), lambda i, j: (b, 0, j))

    r = gate.shape[1]
    return pl.pallas_call(
        _mixout_kernel,
        grid=(m // tm, d // tn),
        in_specs=[pl.BlockSpec((tm, d), lambda i, j: (i, 0)),
                  pl.BlockSpec((tm, MIX_W), lambda i, j: (i, 0)),
                  pl.BlockSpec((tm, tn), lambda i, j: (i, j)),
                  pl.BlockSpec((None, r, tn), lambda i, j: ((i * tm) // rows_per_group, 0, j)),
                  gate_spec(0), gate_spec(1), gate_spec(2),
                  pl.BlockSpec((MIX_W, tn), lambda i, j: (0, j))],
        out_specs=pl.BlockSpec((tm, tn), lambda i, j: (i, j)),
        out_shape=jax.ShapeDtypeStruct((m, d), f32),
        compiler_params=_cparams(("parallel", "parallel")),
        name="mixout",
    )(h, o, x, gate, wg, wg, wg, wo)


def _softmax_pv(s, allowed, v):
    s = jnp.where(allowed, s * SCALE, NEG_INF)
    m = jnp.max(s, axis=-1, keepdims=True)
    p = jnp.exp(s - m)
    denom = jnp.sum(p, axis=-1, keepdims=True)
    return _dot(p.astype(bf16), v) / denom


def _topk_mask(score, adm, n_keep, n_idx_bits):
    skey = _sortable_key(jnp.where(adm, score, NEG_INF))
    rows = score.shape[0]

    def count(pred):
        return jnp.sum(jnp.where(pred, 1.0, 0.0), axis=-1, keepdims=True)

    def vbit(it, thr):
        cand = thr | lax.shift_left(jnp.int32(1), jnp.int32(31) - it)
        return jnp.where(count(skey >= (cand ^ jnp.int32(INT_MIN))) >= n_keep, cand, thr)

    thr = lax.fori_loop(0, 32, vbit, jnp.zeros((rows, 1), jnp.int32)) ^ jnp.int32(INT_MIN)
    gt = skey > thr
    tie = skey == thr
    need = float(n_keep) - count(gt)
    idx = lax.broadcasted_iota(jnp.int32, score.shape, 1)

    def tie_break():
        def ibit(it, c):
            cand = c | lax.shift_left(jnp.int32(1), jnp.int32(n_idx_bits - 1) - it)
            return jnp.where(count(tie & (idx < cand)) < need, cand, c)

        cut = lax.fori_loop(0, n_idx_bits, ibit, jnp.zeros((rows, 1), jnp.int32))
        return jnp.where(tie & (idx <= cut), 1.0, 0.0)

    def no_tie_break():
        return jnp.where(tie, 1.0, 0.0)

    extra = jnp.max(count(tie) - need)
    tie_sel = lax.cond(extra > 0.0, tie_break, no_tie_break)
    return (gt | (tie_sel > 0.5)) & adm


def _attn_prompt_kernel(q_ref, k_ref, v_ref, qi_ref, kk_ref, kwq_ref, o_ref, kmean_ref, *, n_keep):
    tq = q_ref.shape[0]
    t = k_ref.shape[0]
    nb = t // MOBA_BLOCK
    i = pl.program_id(1)
    q_pos = i * tq + lax.broadcasted_iota(jnp.int32, (tq, 1), 0)
    key_pos = lax.broadcasted_iota(jnp.int32, (tq, t), 1)
    causal = key_pos <= q_pos

    e_row = lax.broadcasted_iota(jnp.int32, (LANES, t), 0)
    e_col = lax.broadcasted_iota(jnp.int32, (LANES, t), 1)
    e_ind = jnp.where(e_col // MOBA_BLOCK == e_row, 1.0, 0.0).astype(bf16)

    @pl.when(i == 0)
    def _():
        kmean_ref[...] = _dot(e_ind, k_ref[:, :H_MOBA * HEAD_DIM]) * (1.0 / MOBA_BLOCK)

    own = q_pos // MOBA_BLOCK
    blk = lax.broadcasted_iota(jnp.int32, (tq, LANES), 1)
    past = blk < own
    for h in range(H_MOBA):
        sl = slice(h * HEAD_DIM, (h + 1) * HEAD_DIM)
        qh = q_ref[:, sl]
        km_hi, km_lo = _split_bf16(kmean_ref[:, sl])
        gate = _dot_nt(qh, km_hi) + _dot_nt(qh, km_lo)
        gate = jnp.where(past, gate, NEG_INF)
        rank = jnp.zeros((tq, LANES), jnp.int32)
        for mblk in range(nb):
            col = gate[:, mblk:mblk + 1]
            ahead = (col > gate) | ((col == gate) & (mblk < blk))
            rank = rank + jnp.where(ahead, 1, 0)
        sel = (past & (rank < MOBA_TOPK)) | (blk == own)
        allowed = (_dot(jnp.where(sel, 1.0, 0.0).astype(bf16), e_ind) > 0.5) & causal
        s = _dot_nt(qh, k_ref[:, sl])
        o_ref[:, sl] = _softmax_pv(s, allowed, v_ref[:, sl]).astype(o_ref.dtype)

    lane = lax.broadcasted_iota(jnp.int32, (tq, LANES), 1)
    kk = kk_ref[...]
    kwq = kwq_ref[...]
    score = jnp.zeros((tq, t), f32)
    for g in range(IDX_W // LANES):
        chunk = qi_ref[:, g * LANES:(g + 1) * LANES]
        for half in range(LANES // IDX_DIM):
            hidx = g * (LANES // IDX_DIM) + half
            in_half = (lane >= half * IDX_DIM) & (lane < (half + 1) * IDX_DIM)
            qm = jnp.where(in_half, chunk, jnp.zeros_like(chunk))
            rel = jnp.maximum(_dot_nt(qm, kk), 0.0) * (IDX_DIM ** -0.5)
            score = score + kwq[:, IDX_DIM + hidx:IDX_DIM + hidx + 1] * rel
    sel = _topk_mask(score, causal, n_keep, max(1, (t - 1).bit_length()))
    for h in range(H_MOBA, H_MOBA + H_DSA):
        sl = slice(h * HEAD_DIM, (h + 1) * HEAD_DIM)
        s = _dot_nt(q_ref[:, sl], k_ref[:, sl])
        o_ref[:, sl] = _softmax_pv(s, sel, v_ref[:, sl]).astype(o_ref.dtype)

    kb = tq
    u_row = lax.broadcasted_iota(jnp.int32, (kb, kb), 0)
    u_col = lax.broadcasted_iota(jnp.int32, (kb, kb), 1)
    u_tri = jnp.where(u_row > u_col, 1.0, 0.0).astype(bf16)
    blk_pos = lax.broadcasted_iota(jnp.int32, (tq, kb), 1)
    for h in range(H_MOBA + H_DSA, N_HEADS):
        sl = slice(h * HEAD_DIM, (h + 1) * HEAD_DIM)
        qh = q_ref[:, sl]

        def body(step, carry_acc, sl=sl, qh=qh):
            carry, acc = carry_acc
            start = pl.multiple_of((i - step) * kb, kb)
            kj = k_ref[pl.ds(start, kb), sl]
            vj = v_ref[pl.ds(start, kb), sl]
            z = _dot_nt(qh, kj) * SCALE
            strict = (start + blk_pos) < q_pos
            lf = jnp.where(strict, -_softplus(z), 0.0)
            lf_hi, lf_lo = _split_bf16(lf)
            later = _dot(lf_hi, u_tri) + _dot(lf_lo, u_tri) + carry
            w = jnp.where(strict, jnp.exp(z + lf + later), 0.0)
            acc = acc + _dot(w.astype(bf16), vj)
            return carry + jnp.sum(lf, axis=-1, keepdims=True), acc

        _, acc = lax.fori_loop(0, i + 1, body, (jnp.zeros((tq, 1), f32), jnp.zeros((tq, HEAD_DIM), f32)))
        o_ref[:, sl] = acc.astype(o_ref.dtype)


def _attn_prompt(q16, k16, v16, qi16, kk16, kw32, batch, t, n_keep):
    tq = min(256, t)
    nt = t // tq
    kern = functools.partial(_attn_prompt_kernel, n_keep=n_keep)
    return pl.pallas_call(
        kern,
        grid=(batch, nt),
        in_specs=[pl.BlockSpec((tq, MIX_W), lambda b, i: (b * nt + i, 0)),
                  pl.BlockSpec((t, MIX_W), lambda b, i: (b, 0)),
                  pl.BlockSpec((t, MIX_W), lambda b, i: (b, 0)),
                  pl.BlockSpec((tq, IDX_W), lambda b, i: (b * nt + i, 0)),
                  pl.BlockSpec((t, LANES), lambda b, i: (b, 0)),
                  pl.BlockSpec((tq, LANES), lambda b, i: (b * nt + i, 0))],
        out_specs=pl.BlockSpec((tq, MIX_W), lambda b, i: (b * nt + i, 0)),
        out_shape=jax.ShapeDtypeStruct((batch * t, MIX_W), bf16),
        scratch_shapes=[pltpu.VMEM((LANES, H_MOBA * HEAD_DIM), f32)],
        compiler_params=_cparams(("parallel", "arbitrary")),
        name="attn_prompt",
    )(q16, k16, v16, qi16, kk16, kw32)


def _samp_idx_kernel(pt_ref, qi_ref, wi_ref, kinew_ref, *refs, n_pg, n_keep):
    kidx_refs = refs[:n_pg]
    mask_ref, selfsel_ref, score_ref = refs[n_pg:]
    s = pl.program_id(1)
    qi = qi_ref[...]
    wi = wi_ref[...]
    for g in range(n_pg):
        kp = kidx_refs[g][...].astype(bf16)
        rel = jnp.maximum(_dot_nt(qi, kp), 0.0) * (IDX_DIM ** -0.5)
        score_ref[pl.ds(s * n_pg + g, 1), :] = jnp.sum(wi * rel, axis=0, keepdims=True)

    @pl.when(s == pl.num_programs(1) - 1)
    def _():
        sc = score_ref[...]
        n_pages, page = sc.shape
        ki_new = kinew_ref[...].astype(bf16).astype(f32)
        rel_self = jnp.maximum(jnp.sum(qi.astype(f32) * ki_new, axis=-1, keepdims=True), 0.0) * (IDX_DIM ** -0.5)
        sc_self = jnp.sum(wi[:, :1] * rel_self, axis=0, keepdims=True)
        skey = _sortable_key(sc)
        skey_self = _sortable_key(sc_self)

        def count(pred):
            return jnp.sum(jnp.sum(jnp.where(pred, 1.0, 0.0), axis=1, keepdims=True), axis=0, keepdims=True)

        def vbit(it, thr):
            cand = thr | lax.shift_left(jnp.int32(1), jnp.int32(31) - it)
            cs = cand ^ jnp.int32(INT_MIN)
            n_ge = count(skey >= cs) + jnp.where(skey_self >= cs, 1.0, 0.0)
            return jnp.where(n_ge >= n_keep, cand, thr)

        thr = lax.fori_loop(0, 32, vbit, jnp.zeros((1, 1), jnp.int32)) ^ jnp.int32(INT_MIN)
        gt = skey > thr
        tie = skey == thr
        gt_self = skey_self > thr
        tie_self = skey_self == thr
        need = float(n_keep) - count(gt) - jnp.where(gt_self, 1.0, 0.0)
        idx = (lax.broadcasted_iota(jnp.int32, sc.shape, 0) * page
               + lax.broadcasted_iota(jnp.int32, sc.shape, 1))
        n_bits = max(1, (n_pages * page - 1).bit_length())

        def ibit(it, c):
            cand = c | lax.shift_left(jnp.int32(1), jnp.int32(n_bits - 1) - it)
            return jnp.where(count(tie & (idx < cand)) < need, cand, c)

        cut = lax.fori_loop(0, n_bits, ibit, jnp.zeros((1, 1), jnp.int32))
        tie_taken = tie & (idx <= cut)
        mask_ref[...] = jnp.where(gt | tie_taken, 1.0, 0.0)
        self_taken = gt_self | (tie_self & (count(tie) < need))
        selfsel_ref[...] = jnp.broadcast_to(jnp.where(self_taken, 1.0, 0.0), selfsel_ref.shape)


def _samp_idx(page_table_flat, qi, wi, ki_new, cache_kidx, layer, n_pages, n_keep):
    bd = qi.shape[0]
    page = cache_kidx.shape[2]
    n_pg = min(16, n_pages)
    steps = n_pages // n_pg

    def kidx_spec(g):
        return pl.BlockSpec((None, None, page, IDX_DIM),
                            lambda b, s, pt: (pt[b * n_pages + s * n_pg + g], layer, 0, 0))

    kern = functools.partial(_samp_idx_kernel, n_pg=n_pg, n_keep=n_keep)
    grid_spec = pltpu.PrefetchScalarGridSpec(
        num_scalar_prefetch=1,
        grid=(bd, steps),
        in_specs=[pl.BlockSpec((None, N_IDX_HEADS, IDX_DIM), lambda b, s, pt: (b, 0, 0)),
                  pl.BlockSpec((None, N_IDX_HEADS, LANES), lambda b, s, pt: (b, 0, 0)),
                  pl.BlockSpec((None, 1, IDX_DIM), lambda b, s, pt: (b, 0, 0))]
                 + [kidx_spec(g) for g in range(n_pg)],
        out_specs=[pl.BlockSpec((None, n_pages, page), lambda b, s, pt: (b, 0, 0)),
                   pl.BlockSpec((None, 1, LANES), lambda b, s, pt: (b, 0, 0))],
        scratch_shapes=[pltpu.VMEM((n_pages, page), f32)],
    )
    return pl.pallas_call(
        kern,
        grid_spec=grid_spec,
        out_shape=[jax.ShapeDtypeStruct((bd, n_pages, page), f32),
                   jax.ShapeDtypeStruct((bd, 1, LANES), f32)],
        compiler_params=_cparams(("parallel", "arbitrary")),
        name="samp_idx",
    )(page_table_flat, qi, wi, ki_new, *([cache_kidx] * n_pg))


def _head_rows(page_ref, h, page):
    return page_ref[pl.ds(h, page, stride=N_HEADS), :].astype(bf16)


def _samp_attn_kernel(pt_ref, q_ref, knew_ref, vnew_ref, mask_ref, selfsel_ref, *refs, n_pg):
    k_refs = refs[:n_pg]
    v_refs = refs[n_pg:2 * n_pg]
    o_ref, m_ref, l_ref, acc_ref, carry_ref, pm_ref, pl_ref, pg_ref, pacc_ref = refs[2 * n_pg:]
    s = pl.program_id(1)
    n_pages, page = mask_ref.shape
    shape = (N_HEADS, page)
    row = lax.broadcasted_iota(jnp.int32, shape, 0)
    is_dsa = (row >= H_MOBA) & (row < H_MOBA + H_DSA)
    is_sb = row >= H_MOBA + H_DSA
    u_row = lax.broadcasted_iota(jnp.int32, (page, page), 0)
    u_col = lax.broadcasted_iota(jnp.int32, (page, page), 1)
    u_tri = jnp.where(u_row > u_col, 1.0, 0.0).astype(bf16)

    @pl.when(s == 0)
    def _():
        m_ref[...] = jnp.full(shape, NEG_INF, f32)
        l_ref[...] = jnp.zeros(shape, f32)
        acc_ref[...] = jnp.zeros((N_HEADS, HEAD_DIM), f32)
        carry_ref[...] = jnp.zeros(shape, f32)

    row_d = lax.broadcasted_iota(jnp.int32, (N_HEADS, HEAD_DIM), 0)
    for g in range(n_pg):
        pg = n_pages - 1 - (s * n_pg + g)
        sraw = jnp.zeros(shape, f32)
        for h in range(N_HEADS):
            qh = jnp.broadcast_to(q_ref[h:h + 1, :], (N_HEADS, HEAD_DIM))
            sraw = jnp.where(row == h, _dot_nt(qh, _head_rows(k_refs[g], h, page)), sraw)
        z = sraw * SCALE
        mp = jnp.max(z, axis=-1, keepdims=True)
        e_a = jnp.exp(z - mp)
        dsel = mask_ref[pl.ds(pg, 1), :] > 0.5
        zd = jnp.where(dsel, z, NEG_INF)
        m_old = m_ref[...]
        m_new = jnp.maximum(m_old, jnp.max(zd, axis=-1, keepdims=True))
        m_safe = jnp.where(m_new == NEG_INF, 0.0, m_new)
        e_d = jnp.exp(zd - m_safe)
        alpha = jnp.where(is_dsa, jnp.exp(m_old - m_safe), 1.0)
        lf = -_softplus(z)
        lf_hi, lf_lo = _split_bf16(lf)
        later = _dot(lf_hi, u_tri) + _dot(lf_lo, u_tri) + carry_ref[...]
        w = jnp.exp(z + lf + later)
        p = jnp.where(is_sb, w, jnp.where(is_dsa, e_d, e_a))
        pb = p.astype(bf16)
        pv = jnp.zeros((N_HEADS, HEAD_DIM), f32)
        for h in range(N_HEADS):
            ph = jnp.broadcast_to(pb[h:h + 1, :], shape)
            pv = jnp.where(row_d == h, _dot(ph, _head_rows(v_refs[g], h, page)), pv)
        m_ref[...] = m_new
        l_ref[...] = alpha * l_ref[...] + jnp.sum(e_d, axis=-1, keepdims=True)
        acc_ref[...] = alpha * acc_ref[...] + pv
        carry_ref[...] = carry_ref[...] + jnp.sum(lf, axis=-1, keepdims=True)
        pm_ref[pg] = jnp.broadcast_to(mp, shape)
        pl_ref[pg] = jnp.broadcast_to(jnp.sum(e_a, axis=-1, keepdims=True), shape)
        pg_ref[pg] = jnp.broadcast_to(jnp.sum(sraw, axis=-1, keepdims=True), shape)
        pacc_ref[pg] = pv

    @pl.when(s == pl.num_programs(1) - 1)
    def _():
        k_new = knew_ref[...].astype(bf16).astype(f32)
        v_new = vnew_ref[...].astype(bf16).astype(f32)
        z_self = jnp.sum(q_ref[...].astype(f32) * k_new, axis=-1, keepdims=True) * SCALE
        z_self = jnp.broadcast_to(z_self, shape)
        ppb = MOBA_BLOCK // page
        nb = n_pages // ppb
        gate = jnp.sum(pg_ref[...].reshape(nb, ppb, N_HEADS, page), axis=1) * (1.0 / MOBA_BLOCK)
        blk = lax.broadcasted_iota(jnp.int32, gate.shape, 0)
        sel = jnp.zeros(gate.shape, f32)
        for _ in range(min(MOBA_TOPK, nb)):
            best = jnp.max(gate, axis=0, keepdims=True)
            first = jnp.min(jnp.where(gate == best, blk, nb), axis=0, keepdims=True)
            pick = blk == first
            sel = jnp.where(pick, 1.0, sel)
            gate = jnp.where(pick, NEG_INF, gate)
        selp = jnp.broadcast_to(sel[:, None], (nb, ppb, N_HEADS, page)).reshape(n_pages, N_HEADS, page) > 0.5
        pm = pm_ref[...]
        m_a = jnp.maximum(jnp.max(jnp.where(selp, pm, NEG_INF), axis=0), z_self)
        wts = jnp.where(selp, jnp.exp(pm - m_a), 0.0)
        e_self = jnp.exp(z_self - m_a)
        l_a = jnp.sum(wts * pl_ref[...], axis=0) + e_self
        o_a = (jnp.sum(wts * pacc_ref[...], axis=0) + e_self * v_new) / l_a
        take = jnp.broadcast_to(selfsel_ref[...] > 0.5, shape)
        m_old = m_ref[...]
        m_d = jnp.where(take, jnp.maximum(m_old, z_self), m_old)
        m_safe = jnp.where(m_d == NEG_INF, 0.0, m_d)
        a_d = jnp.exp(m_old - m_safe)
        e_self_d = jnp.where(take, jnp.exp(z_self - m_safe), 0.0)
        o_d = (a_d * acc_ref[...] + e_self_d * v_new) / (a_d * l_ref[...] + e_self_d)
        o_ref[...] = jnp.where(is_sb, acc_ref[...], jnp.where(is_dsa, o_d, o_a))


def _samp_attn(page_table_flat, q16, k_new, v_new, mask, selfsel, cache_k, cache_v, layer):
    bd = q16.shape[0]
    n_pages, page = mask.shape[1:]
    n_pg = min(4, n_pages)
    steps = n_pages // n_pg

    def page_spec(g):
        return pl.BlockSpec((None, None, page * N_HEADS, HEAD_DIM),
                            lambda b, s, pt: (pt[b * n_pages + n_pages - 1 - (s * n_pg + g)], layer, 0, 0))

    def per_seq(shape):
        return pl.BlockSpec((None,) + shape, lambda b, s, pt: (b,) + (0,) * len(shape))

    kern = functools.partial(_samp_attn_kernel, n_pg=n_pg)
    stat = pltpu.VMEM((N_HEADS, page), f32)
    page_stat = pltpu.VMEM((n_pages, N_HEADS, page), f32)
    grid_spec = pltpu.PrefetchScalarGridSpec(
        num_scalar_prefetch=1,
        grid=(bd, steps),
        in_specs=[per_seq((N_HEADS, HEAD_DIM)), per_seq((N_HEADS, HEAD_DIM)),
                  per_seq((N_HEADS, HEAD_DIM)), per_seq((n_pages, page)), per_seq((1, LANES))]
                 + [page_spec(g) for g in range(n_pg)] * 2,
        out_specs=per_seq((N_HEADS, HEAD_DIM)),
        scratch_shapes=[stat, stat, pltpu.VMEM((N_HEADS, HEAD_DIM), f32), stat,
                        page_stat, page_stat, page_stat, page_stat],
    )
    return pl.pallas_call(
        kern,
        grid_spec=grid_spec,
        out_shape=jax.ShapeDtypeStruct((bd, N_HEADS, HEAD_DIM), f32),
        compiler_params=_cparams(("parallel", "arbitrary")),
        name="samp_attn",
    )(page_table_flat, q16, k_new, v_new, mask, selfsel, *([cache_k] * n_pg), *([cache_v] * n_pg))


def _rope_tables(pos, rot_dim, period, valid_lanes):
    half = rot_dim // 2
    inv_freq = jnp.power(ROPE_THETA, -jnp.arange(half, dtype=f32) * (2.0 / rot_dim))
    ang = pos.astype(f32)[:, None] * inv_freq[None, :]
    cos, sin = jnp.cos(ang), jnp.sin(ang)
    lane = jnp.arange(LANES)
    in_head = lane % period
    live = lane < valid_lanes
    first = (in_head < half) & live
    second = (in_head >= half) & (in_head < rot_dim) & live
    fidx = jnp.clip(in_head - jnp.where(second, half, 0), 0, half - 1)
    c = jnp.where((first | second)[None, :], cos[:, fidx], 1.0)
    a = jnp.where(first[None, :], -sin[:, fidx], 0.0)
    b = jnp.where(second[None, :], sin[:, fidx], 0.0)
    return c, a, b


def _head_vectors(g):
    gain = jnp.concatenate([jnp.tile(g[0], H_MOBA), jnp.tile(g[1], H_DSA), jnp.ones((H_SB * HEAD_DIM,), f32)])
    flag = jnp.concatenate([jnp.ones(((H_MOBA + H_DSA) * HEAD_DIM,), f32), jnp.zeros((H_SB * HEAD_DIM,), f32)])
    return gain[None, :], flag[None, :]


def _layer_weights(w_in_l, w_out_l, w1_l, w3_l, w2_l, d):
    o3 = 3 * MIX_W
    o5 = o3 + IDX_W + IDX_DIM + N_IDX_HEADS
    w_idx = jnp.pad(w_in_l[:, o3:o5], ((0, 0), (0, IDX_PAD_W - (o5 - o3))))
    return dict(
        wq=w_in_l[:, :MIX_W].astype(bf16),
        wk=w_in_l[:, MIX_W:2 * MIX_W].astype(bf16),
        wv=w_in_l[:, 2 * MIX_W:o3].astype(bf16),
        widx=w_idx.astype(bf16),
        wg=w_in_l[:, o5:].astype(bf16).reshape(d, N_BRANCH, d).transpose(1, 0, 2),
        wo=w_out_l.astype(bf16),
        w1=w1_l.astype(bf16), w3=w3_l.astype(bf16), w2=w2_l.astype(bf16),
    )


def _dense_front(x, mod, lw, norm_g_l, tm, rpg):
    h = _modnorm(x, norm_g_l[0:1], mod[0], mod[1], tm, rpg)
    x = _ffn(h, x, mod[2], lw["w1"][0], lw["w3"][0], lw["w2"][0], tm, rpg)
    return x, _modnorm(x, norm_g_l[1:2], mod[3], mod[4], tm, rpg)


def _dense_back(x, h, o, mod, lw, norm_g_l, tm, rpg):
    x = _mixout(h, o, x, mod[5], lw["wg"], lw["wo"], tm, rpg)
    h = _modnorm(x, norm_g_l[2:3], mod[6], mod[7], tm, rpg)
    return _ffn(h, x, mod[8], lw["w1"][1], lw["w3"][1], lw["w2"][1], tm, rpg)


def _projections(h, lw, qv, kv, head_tables, idx_tables, tm):
    (q16,) = _proj_qk(h, lw["wq"], qv[0], qv[1], head_tables, tm, False)
    k16, k32 = _proj_qk(h, lw["wk"], kv[0], kv[1], head_tables, tm, True)
    v16, v32 = _proj_v(h, lw["wv"], tm)
    qi16, kw32, kk16 = _proj_idx(h, lw["widx"], idx_tables, tm)
    return q16, k16, k32, v16, v32, qi16, kw32, kk16


def kernel(x_prompt, x_sample, cache_k, cache_v, cache_kidx, page_table, c_prompt, c_sample,
           w_ada, b_ada, norm_g, w_in, q_norm_g, k_norm_g, w_out, ffn_w1, ffn_w3, ffn_w2):
    batch, t, d = x_prompt.shape
    bd, dec_seq, _ = x_sample.shape
    assert dec_seq == 1 and bd <= SAMPLE_ROWS
    depth = w_ada.shape[0]
    n_phys, _, page, _, _ = cache_k.shape
    n_pages = page_table.shape[1]
    past_len = n_pages * page
    assert t % MOBA_BLOCK == 0 and past_len % MOBA_BLOCK == 0 and MOBA_BLOCK % page == 0
    keep_p = min(DSA_TOPK, t // 4)
    keep_s = min(DSA_TOPK, (past_len + dec_seq) // 4)
    m_p = batch * t
    tm_p = min(512, t)

    c_rows = batch + SAMPLE_ROWS
    c_all = jnp.zeros((-(-c_rows // 8) * 8, d), f32).at[:batch].set(c_prompt).at[batch:batch + bd].set(c_sample)
    mod_all = _ada(c_all, w_ada, b_ada)

    pos_p = jnp.arange(t, dtype=jnp.int32)
    pos_s = jnp.full((SAMPLE_ROWS,), past_len, jnp.int32)
    head_tab_p = _rope_tables(pos_p, ROPE_DIM, HEAD_DIM, LANES)
    head_tab_s = _rope_tables(pos_s, ROPE_DIM, HEAD_DIM, LANES)
    idx_tab_p = _rope_tables(pos_p, IDX_ROPE_DIM, IDX_DIM, LANES) + _rope_tables(pos_p, IDX_ROPE_DIM, IDX_DIM, IDX_DIM)
    idx_tab_s = _rope_tables(pos_s, IDX_ROPE_DIM, IDX_DIM, LANES) + _rope_tables(pos_s, IDX_ROPE_DIM, IDX_DIM, IDX_DIM)

    cache_k2 = cache_k.reshape(n_phys, depth, page * N_HEADS, HEAD_DIM)
    cache_v2 = cache_v.reshape(n_phys, depth, page * N_HEADS, HEAD_DIM)
    pt_flat = page_table.reshape(-1).astype(jnp.int32)

    xp = x_prompt.reshape(m_p, d)
    xs = jnp.zeros((SAMPLE_ROWS, d), f32).at[:bd].set(x_sample.reshape(bd, d))
    outs = [[] for _ in range(6)]
    for l in range(depth):
        lw = _layer_weights(w_in[l], w_out[l], ffn_w1[l], ffn_w3[l], ffn_w2[l], d)
        qv = _head_vectors(q_norm_g[l])
        kv = _head_vectors(k_norm_g[l])
        mod_l = mod_all[l].reshape(-1, N_MOD, d)
        mod_p = [mod_l[:batch, j].reshape(batch, 1, d) for j in range(N_MOD)]
        mod_s = [mod_l[batch:batch + SAMPLE_ROWS, j].reshape(1, SAMPLE_ROWS, d) for j in range(N_MOD)]

        xp, h = _dense_front(xp, mod_p, lw, norm_g[l], tm_p, t)
        q16, k16, k32, v16, v32, qi16, kw32, kk16 = _projections(h, lw, qv, kv, head_tab_p, idx_tab_p, tm_p)
        o16 = _attn_prompt(q16, k16, v16, qi16, kk16, kw32, batch, t, keep_p)
        xp = _dense_back(xp, h, o16, mod_p, lw, norm_g[l], tm_p, t)
        outs[0].append(k32.reshape(batch, t, N_HEADS, HEAD_DIM))
        outs[1].append(v32.reshape(batch, t, N_HEADS, HEAD_DIM))
        outs[2].append(kw32[:, :IDX_DIM].reshape(batch, t, IDX_DIM))

        xs, h = _dense_front(xs, mod_s, lw, norm_g[l], SAMPLE_ROWS, SAMPLE_ROWS)
        q16, k16, k32, v16, v32, qi16, kw32, kk16 = _projections(h, lw, qv, kv, head_tab_s, idx_tab_s, SAMPLE_ROWS)
        q_h = q16[:bd].reshape(bd, N_HEADS, HEAD_DIM)
        k_new = k32[:bd].reshape(bd, N_HEADS, HEAD_DIM)
        v_new = v32[:bd].reshape(bd, N_HEADS, HEAD_DIM)
        ki_new = kw32[:bd, :IDX_DIM]
        wi = jnp.broadcast_to(kw32[:bd, IDX_DIM:IDX_DIM + N_IDX_HEADS, None], (bd, N_IDX_HEADS, LANES))
        mask, selfsel = _samp_idx(pt_flat, qi16[:bd].reshape(bd, N_IDX_HEADS, IDX_DIM), wi,
                                  ki_new.reshape(bd, 1, IDX_DIM), cache_kidx, l, n_pages, keep_s)
        o_s = _samp_attn(pt_flat, q_h, k_new, v_new, mask, selfsel, cache_k2, cache_v2, l)
        o16 = jnp.zeros((SAMPLE_ROWS, MIX_W), bf16).at[:bd].set(o_s.reshape(bd, MIX_W).astype(bf16))
        xs = _dense_back(xs, h, o16, mod_s, lw, norm_g[l], SAMPLE_ROWS, SAMPLE_ROWS)
        outs[3].append(k_new.reshape(bd, 1, N_HEADS, HEAD_DIM))
        outs[4].append(v_new.reshape(bd, 1, N_HEADS, HEAD_DIM))
        outs[5].append(ki_new.reshape(bd, 1, IDX_DIM))

    stacked = [jnp.stack(o, axis=1) for o in outs]
    return (xp.reshape(batch, t, d), xs[:bd].reshape(bd, 1, d), *stacked)
```

```python
import functools

import jax
import jax.numpy as jnp
from jax import lax
from jax.experimental import pallas as pl
from jax.experimental.pallas import tpu as pltpu

HEAD_DIM = 128
H_MOBA = 6
H_DSA = 4
H_SB = 6
N_HEADS = H_MOBA + H_DSA + H_SB
MIX_W = N_HEADS * HEAD_DIM
ROPE_DIM = HEAD_DIM // 4
ROPE_THETA = 500000.0
MOBA_BLOCK = 256
MOBA_TOPK = 3
N_IDX_HEADS = 16
IDX_DIM = 64
IDX_ROPE_DIM = IDX_DIM // 4
IDX_W = N_IDX_HEADS * IDX_DIM
IDX_PAD_W = IDX_W + 128
DSA_TOPK = 256
N_BRANCH = 3
N_MOD = 9
EPS = 1e-6
LANES = 128
SAMPLE_ROWS = 16
VMEM_LIMIT = 56 * 1024 * 1024
NEG_INF = float("-inf")
INT_MIN = -(2 ** 31)
SCALE = HEAD_DIM ** -0.5

f32 = jnp.float32
bf16 = jnp.bfloat16


def _cparams(sem):
    return pltpu.CompilerParams(dimension_semantics=sem, vmem_limit_bytes=VMEM_LIMIT)


def _dot(a, b):
    return jnp.dot(a, b, preferred_element_type=f32)


def _dot_nt(a, b):
    return lax.dot_general(a, b, (((1,), (1,)), ((), ())), preferred_element_type=f32)


def _split_bf16(x):
    hi = x.astype(bf16)
    lo = (x - hi.astype(f32)).astype(bf16)
    return hi, lo


def _softplus(z):
    return jnp.maximum(z, 0.0) + jnp.log(1.0 + jnp.exp(-jnp.abs(z)))


def _suffix_sums(lf, u_tri2):
    lf_hi, lf_lo = _split_bf16(lf)
    return _dot(jnp.concatenate([lf_hi, lf_lo], axis=1), u_tri2)


def _tri2(n):
    row = lax.broadcasted_iota(jnp.int32, (2 * n, n), 0)
    col = lax.broadcasted_iota(jnp.int32, (2 * n, n), 1)
    return jnp.where((row % n) > col, 1.0, 0.0).astype(bf16)


def _sortable_key(x):
    bits = pltpu.bitcast(x + 0.0, jnp.int32)
    return bits ^ ((bits >> 31) & jnp.int32(0x7FFFFFFF))


def _ada_kernel(c_ref, w_ref, b_ref, o_ref):
    c = c_ref[...]
    a = c * jax.nn.sigmoid(c)
    a_hi, a_lo = _split_bf16(a)
    w_hi, w_lo = _split_bf16(w_ref[...])
    o_ref[...] = _dot(a_hi, w_hi) + _dot(a_lo, w_hi) + _dot(a_hi, w_lo) + b_ref[...]


def _ada(c_all, w_ada, b_ada):
    depth, d, n = w_ada.shape
    rows = c_all.shape[0]
    tn = d // 2 if d >= 2 * LANES else d
    return pl.pallas_call(
        _ada_kernel,
        grid=(depth, n // tn),
        in_specs=[pl.BlockSpec((rows, d), lambda l, j: (0, 0)),
                  pl.BlockSpec((None, d, tn), lambda l, j: (l, 0, j)),
                  pl.BlockSpec((None, 1, tn), lambda l, j: (l, 0, j))],
        out_specs=pl.BlockSpec((None, rows, tn), lambda l, j: (l, 0, j)),
        out_shape=jax.ShapeDtypeStruct((depth, rows, n), f32),
        compiler_params=_cparams(("parallel", "parallel")),
        name="ada",
    )(c_all, w_ada, b_ada.reshape(depth, 1, n))


def _modnorm_kernel(x_ref, g_ref, shift_ref, scale_ref, o_ref):
    x = x_ref[...]
    ms = jnp.mean(x * x, axis=-1, keepdims=True)
    y = x * lax.rsqrt(ms + EPS) * g_ref[...]
    o_ref[...] = (y * (1.0 + scale_ref[...]) + shift_ref[...]).astype(o_ref.dtype)


def _mod_spec(mod, tm, rows_per_group):
    _, r, d = mod.shape
    return pl.BlockSpec((None, r, d), lambda i, *_: ((i * tm) // rows_per_group, 0, 0))


def _modnorm(x, g, shift, scale, tm, rows_per_group):
    m, d = x.shape
    return pl.pallas_call(
        _modnorm_kernel,
        grid=(m // tm,),
        in_specs=[pl.BlockSpec((tm, d), lambda i: (i, 0)),
                  pl.BlockSpec((1, d), lambda i: (0, 0)),
                  _mod_spec(shift, tm, rows_per_group),
                  _mod_spec(scale, tm, rows_per_group)],
        out_specs=pl.BlockSpec((tm, d), lambda i: (i, 0)),
        out_shape=jax.ShapeDtypeStruct((m, d), bf16),
        compiler_params=_cparams(("parallel",)),
        name="modnorm",
    )(x, g, shift, scale)


def _ffn_kernel(h_ref, x_ref, gate_ref, w1_ref, w3_ref, w2_ref, o_ref, acc_ref):
    f = pl.program_id(1)

    @pl.when(f == 0)
    def _():
        acc_ref[...] = jnp.zeros_like(acc_ref)

    h = h_ref[...]
    a = _dot(h, w1_ref[...])
    b = _dot(h, w3_ref[...])
    u = (a * jax.nn.sigmoid(a) * b).astype(bf16)
    acc_ref[...] += _dot(u, w2_ref[...])

    @pl.when(f == pl.num_programs(1) - 1)
    def _():
        o_ref[...] = x_ref[...] + 0.5 * gate_ref[...] * acc_ref[...]


def _ffn(h, x, gate, w1, w3, w2, layer, which, tm, rows_per_group):
    m, d = x.shape
    dff = w1.shape[-1]
    tf = min(512, dff)
    return pl.pallas_call(
        _ffn_kernel,
        grid=(m // tm, dff // tf),
        in_specs=[pl.BlockSpec((tm, d), lambda i, f: (i, 0)),
                  pl.BlockSpec((tm, d), lambda i, f: (i, 0)),
                  _mod_spec(gate, tm, rows_per_group),
                  pl.BlockSpec((None, None, d, tf), lambda i, f: (layer, which, 0, f)),
                  pl.BlockSpec((None, None, d, tf), lambda i, f: (layer, which, 0, f)),
                  pl.BlockSpec((None, None, tf, d), lambda i, f: (layer, which, f, 0))],
        out_specs=pl.BlockSpec((tm, d), lambda i, f: (i, 0)),
        out_shape=jax.ShapeDtypeStruct((m, d), f32),
        scratch_shapes=[pltpu.VMEM((tm, d), f32)],
        compiler_params=_cparams(("parallel", "arbitrary")),
        name="ffn",
    )(h, x, gate, w1, w3, w2)


def _rope(x, c, a, b, shift):
    return x * c + pltpu.roll(x, LANES - shift, 1) * a + pltpu.roll(x, shift, 1) * b


def _proj_qk_kernel(h_ref, w_ref, gain_ref, flag_ref, c_ref, a_ref, b_ref, *rest, out_scale, cache_rows):
    o_refs = rest[-2:] if cache_rows or len(rest) == 2 else rest[-1:]
    acc = _dot(h_ref[...], w_ref[...])
    c, a, b = c_ref[...], a_ref[...], b_ref[...]
    for hh in range(acc.shape[1] // HEAD_DIM):
        sl = slice(hh * HEAD_DIM, (hh + 1) * HEAD_DIM)
        x = acc[:, sl]
        ms = jnp.mean(x * x, axis=-1, keepdims=True)
        xn = x * lax.rsqrt(ms + EPS) * gain_ref[:, sl]
        out = jnp.where(flag_ref[:, sl] > 0.0, _rope(xn, c, a, b, ROPE_DIM // 2), x)
        if out_scale != 1.0:
            out = out * out_scale
        o_refs[0][:, sl] = out.astype(bf16)
        if cache_rows:
            o_refs[1][pl.ds(hh, cache_rows, stride=N_HEADS), :] = out
        elif len(o_refs) == 2:
            o_refs[1][:, sl] = out


def _table_spec(table, tm):
    nblk = table.shape[0] // tm
    return pl.BlockSpec((tm, LANES), lambda i, *_: (i % nblk, 0))


def _cache_out(m, tm, layer, cache_dims, prev):
    batch, depth, t = cache_dims
    nt = t // tm
    shape = jax.ShapeDtypeStruct((batch, depth, t * N_HEADS, HEAD_DIM), f32)
    spec = pl.BlockSpec((None, None, tm * N_HEADS, HEAD_DIM), lambda i, j: (i // nt, layer, i % nt, 0))
    extra_in = [] if prev is None else [prev]
    extra_specs = [] if prev is None else [pl.BlockSpec(memory_space=pl.ANY)]
    return shape, spec, extra_in, extra_specs


def _proj_qk(h, w_in, layer, col0, gain, flag, tables, tm, out_scale, f32_out, cache_dims=None, prev=None):
    m, d = h.shape
    n = MIX_W
    tn = n
    assert col0 % tn == 0
    out_shape = [jax.ShapeDtypeStruct((m, n), bf16)]
    out_specs = [pl.BlockSpec((tm, tn), lambda i, j: (i, j))]
    extra_in, extra_specs, aliases = [], [], {}
    if f32_out == "plain":
        out_shape.append(jax.ShapeDtypeStruct((m, n), f32))
        out_specs.append(pl.BlockSpec((tm, tn), lambda i, j: (i, j)))
    elif f32_out == "cache":
        shape, spec, extra_in, extra_specs = _cache_out(m, tm, layer, cache_dims, prev)
        out_shape.append(shape)
        out_specs.append(spec)
        if extra_in:
            aliases = {7: 1}
    return pl.pallas_call(
        functools.partial(_proj_qk_kernel, out_scale=out_scale, cache_rows=tm if f32_out == "cache" else 0),
        grid=(m // tm, n // tn),
        in_specs=[pl.BlockSpec((tm, d), lambda i, j: (i, 0)),
                  pl.BlockSpec((None, d, tn), lambda i, j: (layer, 0, col0 // tn + j)),
                  pl.BlockSpec((1, tn), lambda i, j: (0, j)),
                  pl.BlockSpec((1, tn), lambda i, j: (0, j))]
                 + [_table_spec(t, tm) for t in tables] + extra_specs,
        out_specs=out_specs,
        out_shape=out_shape,
        input_output_aliases=aliases,
        compiler_params=_cparams(("parallel", "parallel")),
        name="proj_qk",
    )(h, w_in, gain, flag, *tables, *extra_in)


def _proj_v_kernel(h_ref, w_ref, *rest, cache_rows):
    o16_ref, o32_ref = rest[-2:]
    acc = _dot(h_ref[...], w_ref[...])
    o16_ref[...] = acc.astype(bf16)
    if cache_rows:
        for hh in range(N_HEADS):
            o32_ref[pl.ds(hh, cache_rows, stride=N_HEADS), :] = acc[:, hh * HEAD_DIM:(hh + 1) * HEAD_DIM]
    else:
        o32_ref[...] = acc


def _proj_v(h, w_in, layer, col0, tm, cache_dims=None, prev=None):
    m, d = h.shape
    n = MIX_W
    tn = n
    assert col0 % tn == 0
    extra_in, extra_specs, aliases = [], [], {}
    if cache_dims:
        shape, spec, extra_in, extra_specs = _cache_out(m, tm, layer, cache_dims, prev)
        if extra_in:
            aliases = {2: 1}
    else:
        shape = jax.ShapeDtypeStruct((m, n), f32)
        spec = pl.BlockSpec((tm, tn), lambda i, j: (i, j))
    return pl.pallas_call(
        functools.partial(_proj_v_kernel, cache_rows=tm if cache_dims else 0),
        grid=(m // tm, n // tn),
        in_specs=[pl.BlockSpec((tm, d), lambda i, j: (i, 0)),
                  pl.BlockSpec((None, d, tn), lambda i, j: (layer, 0, col0 // tn + j))] + extra_specs,
        out_specs=[pl.BlockSpec((tm, tn), lambda i, j: (i, j)), spec],
        out_shape=[jax.ShapeDtypeStruct((m, n), bf16), shape],
        input_output_aliases=aliases,
        compiler_params=_cparams(("parallel", "parallel")),
        name="proj_v",
    )(h, w_in, *extra_in)


def _proj_idx_kernel(h_ref, w_ref, ci_ref, ai_ref, bi_ref, ck_ref, ak_ref, bk_ref, qi_ref, kw_ref, kk_ref):
    acc = _dot(h_ref[...], w_ref[...])
    ci, ai, bi = ci_ref[...], ai_ref[...], bi_ref[...]
    half = IDX_ROPE_DIM // 2
    for g in range(IDX_W // LANES):
        sl = slice(g * LANES, (g + 1) * LANES)
        qi_ref[:, sl] = _rope(acc[:, sl], ci, ai, bi, half).astype(bf16)
    chunk = _rope(acc[:, IDX_W:IDX_W + LANES], ck_ref[...], ak_ref[...], bk_ref[...], half)
    lane = lax.broadcasted_iota(jnp.int32, chunk.shape, 1)
    wi_scaled = chunk * (N_IDX_HEADS ** -0.5)
    kw_ref[...] = jnp.where(lane < IDX_DIM, chunk, jnp.where(lane < IDX_DIM + N_IDX_HEADS, wi_scaled, 0.0))
    kk_ref[...] = jnp.where(lane < IDX_DIM, chunk, pltpu.roll(chunk, IDX_DIM, 1)).astype(bf16)


def _proj_idx(h, w, tables, tm):
    m, d = h.shape
    return pl.pallas_call(
        _proj_idx_kernel,
        grid=(m // tm,),
        in_specs=[pl.BlockSpec((tm, d), lambda i: (i, 0)),
                  pl.BlockSpec((d, IDX_PAD_W), lambda i: (0, 0))]
                 + [_table_spec(t, tm) for t in tables],
        out_specs=[pl.BlockSpec((tm, IDX_W), lambda i: (i, 0)),
                   pl.BlockSpec((tm, LANES), lambda i: (i, 0)),
                   pl.BlockSpec((tm, LANES), lambda i: (i, 0))],
        out_shape=[jax.ShapeDtypeStruct((m, IDX_W), bf16),
                   jax.ShapeDtypeStruct((m, LANES), f32),
                   jax.ShapeDtypeStruct((m, LANES), bf16)],
        compiler_params=_cparams(("parallel",)),
        name="proj_idx",
    )(h, w, *tables)


_BRANCH_ROWS = ((0, H_MOBA * HEAD_DIM),
                (H_MOBA * HEAD_DIM, (H_MOBA + H_DSA) * HEAD_DIM),
                ((H_MOBA + H_DSA) * HEAD_DIM, MIX_W))


def _mixout_kernel(h_ref, o_ref, x_ref, gate_ref, wg0_ref, wg1_ref, wg2_ref, wo_ref, out_ref):
    h = h_ref[...]
    y = None
    for (r0, r1), wg_ref in zip(_BRANCH_ROWS, (wg0_ref, wg1_ref, wg2_ref)):
        g = jax.nn.sigmoid(_dot(h, wg_ref[...]))
        term = g * _dot(o_ref[:, r0:r1], wo_ref[r0:r1, :])
        y = term if y is None else y + term
    out_ref[...] = x_ref[...] + gate_ref[...] * y


def _mixout(h, o, x, gate, wg, wo, tm, rows_per_group):
    m, d = x.shape
    tn = min(512, d)

    def gate_spec(b):
        return pl.BlockSpec((None, d, tn), lambda i, j: (b, 0, j))

    r = gate.shape[1]
    return pl.pallas_call(
        _mixout_kernel,
        grid=(m // tm, d // tn),
        in_specs=[pl.BlockSpec((tm, d), lambda i, j: (i, 0)),
                  pl.BlockSpec((tm, MIX_W), lambda i, j: (i, 0)),
                  pl.BlockSpec((tm, tn), lambda i, j: (i, j)),
                  pl.BlockSpec((None, r, tn), lambda i, j: ((i * tm) // rows_per_group, 0, j)),
                  gate_spec(0), gate_spec(1), gate_spec(2),
                  pl.BlockSpec((MIX_W, tn), lambda i, j: (0, j))],
        out_specs=pl.BlockSpec((tm, tn), lambda i, j: (i, j)),
        out_shape=jax.ShapeDtypeStruct((m, d), f32),
        compiler_params=_cparams(("parallel", "parallel")),
        name="mixout",
    )(h, o, x, gate, wg, wg, wg, wo)


BIAS_OFF = -1e30


def _flash_step(qh, kj, vj, bias, state):
    m, l, acc = state
    s = _dot_nt(qh, kj) + bias
    m_new = jnp.maximum(m, jnp.max(s, axis=-1, keepdims=True))
    alpha = jnp.exp(m - m_new)
    p = jnp.exp(s - m_new)
    l = alpha * l + jnp.sum(p, axis=-1, keepdims=True)
    acc = alpha * acc + _dot(p.astype(bf16), vj)
    return m_new, l, acc


def _flash_init(tq, n):
    return tuple((jnp.full((tq, 1), BIAS_OFF, f32), jnp.zeros((tq, 1), f32), jnp.zeros((tq, HEAD_DIM), f32))
                 for _ in range(n))


def _attn_prompt_kernel(q_ref, k_ref, v_ref, qi_ref, kk_ref, kwq_ref, o_ref, kmean_ref, skey_ref, bias_ref, *,
                        n_keep):
    tq = q_ref.shape[0]
    t = k_ref.shape[0]
    kb = tq
    i = pl.program_id(1)
    row = lax.broadcasted_iota(jnp.int32, (tq, kb), 0)
    col = lax.broadcasted_iota(jnp.int32, (tq, kb), 1)
    diag_bias = jnp.where(col <= row, 0.0, BIAS_OFF)

    def head(h):
        return slice(h * HEAD_DIM, (h + 1) * HEAD_DIM)

    def rows_of(j):
        return pl.ds(pl.multiple_of(j * kb, kb), kb)

    @pl.when(i == 0)
    def _():
        e_row = lax.broadcasted_iota(jnp.int32, (LANES, t), 0)
        e_col = lax.broadcasted_iota(jnp.int32, (LANES, t), 1)
        e_ind = jnp.where(e_col // MOBA_BLOCK == e_row, 1.0, 0.0).astype(bf16)
        kmean_ref[...] = _dot(e_ind, k_ref[:, :H_MOBA * HEAD_DIM]) * (1.0 / MOBA_BLOCK)

    blk = lax.broadcasted_iota(jnp.int32, (tq, LANES), 1)
    past = blk < i
    sel_past = []
    for h in range(H_MOBA):
        qh = q_ref[:, head(h)]
        km_hi, km_lo = _split_bf16(kmean_ref[:, head(h)])
        gate = _dot_nt(qh, km_hi) + _dot_nt(qh, km_lo)
        gate = jnp.where(past, gate, NEG_INF)
        rank = jnp.zeros((tq, LANES), f32)
        for mblk in range(t // MOBA_BLOCK):
            g_m = gate[:, mblk:mblk + 1]
            wins_tie = jnp.where(mblk < blk, 1.0, 0.0)
            rank = rank + jnp.where(g_m > gate, 1.0, jnp.where(g_m == gate, wins_tie, 0.0))
        sel_past.append(jnp.where(past, jnp.where(rank < MOBA_TOPK, 1.0, 0.0), 0.0))

    def moba_block(j, states):
        out = []
        for h in range(H_MOBA):
            picked = jnp.sum(jnp.where(blk == j, sel_past[h], 0.0), axis=-1, keepdims=True)
            bias = (picked - 1.0) * (-BIAS_OFF)
            out.append(_flash_step(q_ref[:, head(h)], k_ref[rows_of(j), head(h)], v_ref[rows_of(j), head(h)],
                                   bias, states[h]))
        return tuple(out)

    states = lax.fori_loop(0, i, moba_block, _flash_init(tq, H_MOBA))
    for h in range(H_MOBA):
        _, l, acc = _flash_step(q_ref[:, head(h)], k_ref[rows_of(i), head(h)], v_ref[rows_of(i), head(h)],
                                diag_bias, states[h])
        o_ref[:, head(h)] = (acc / l).astype(o_ref.dtype)

    lane = lax.broadcasted_iota(jnp.int32, (tq, LANES), 1)
    kwq = kwq_ref[...]
    q_idx, w_idx = [], []
    for g in range(IDX_W // LANES):
        chunk = qi_ref[:, g * LANES:(g + 1) * LANES]
        for half in range(LANES // IDX_DIM):
            hidx = g * (LANES // IDX_DIM) + half
            in_half = (lane >= half * IDX_DIM) & (lane < (half + 1) * IDX_DIM)
            q_idx.append(jnp.where(in_half, chunk, jnp.zeros_like(chunk)))
            w_idx.append(kwq[:, IDX_DIM + hidx:IDX_DIM + hidx + 1] * (IDX_DIM ** -0.5))

    def adm_of(j):
        return (j * kb + col) <= (i * kb + row)

    def idx_block(j, carry):
        kkj = kk_ref[rows_of(j), :]
        score = jnp.zeros((tq, kb), f32)
        for qm, wh in zip(q_idx, w_idx):
            score = score + jnp.maximum(_dot_nt(qm, kkj), 0.0) * wh
        skey_ref[j] = jnp.where(adm_of(j), _sortable_key(score), jnp.int32(INT_MIN))
        return carry

    lax.fori_loop(0, i + 1, idx_block, 0)

    nt = t // kb

    def blank(j, carry):
        skey_ref[j] = jnp.full((tq, kb), INT_MIN, jnp.int32)
        return carry

    lax.fori_loop(i + 1, nt, blank, 0)
    n_bits = max(1, (t - 1).bit_length())

    def search(nblk):
        def count(pred_of_block):
            c = jnp.where(pred_of_block(0), 1.0, 0.0)
            for jj in range(1, nblk):
                c = c + jnp.where(pred_of_block(jj), 1.0, 0.0)
            return jnp.sum(c, axis=-1, keepdims=True)

        def vbit(it, thr):
            cand = thr | lax.shift_left(jnp.int32(1), jnp.int32(31) - it)
            cand_s = cand ^ jnp.int32(INT_MIN)
            return jnp.where(count(lambda jj: skey_ref[jj] >= cand_s) >= n_keep, cand, thr)

        thr = lax.fori_loop(0, 32, vbit, jnp.zeros((tq, 1), jnp.int32)) ^ jnp.int32(INT_MIN)
        need = float(n_keep) - count(lambda jj: skey_ref[jj] > thr)
        n_tie = count(lambda jj: skey_ref[jj] == thr)

        def tie_cut():
            def ibit(it, c):
                cand = c | lax.shift_left(jnp.int32(1), jnp.int32(n_bits - 1) - it)
                below = count(lambda jj: (skey_ref[jj] == thr) & ((jj * kb + col) < cand))
                return jnp.where(below < need, cand, c)

            return lax.fori_loop(0, n_bits, ibit, jnp.zeros((tq, 1), jnp.int32))

        cut = lax.cond(jnp.max(n_tie - need) > 0.0, tie_cut, lambda: jnp.full((tq, 1), t, jnp.int32))
        return thr, cut

    group = max(1, nt // 4)
    widths = list(range(group, nt + 1, group))

    def pick(widths):
        if len(widths) == 1:
            return search(widths[0])
        return lax.cond(i < widths[0], lambda: search(widths[0]), lambda: pick(widths[1:]))

    thr, cut = pick(widths)

    def bias_block(j, carry):
        key = skey_ref[j]
        idx_ok = (j * kb + col) <= cut
        taken = jnp.where(key > thr, 0.0, jnp.where(key == thr, jnp.where(idx_ok, 0.0, BIAS_OFF), BIAS_OFF))
        bias_ref[j] = jnp.where(adm_of(j), taken, BIAS_OFF)
        return carry

    lax.fori_loop(0, i + 1, bias_block, 0)

    def dsa_block(j, states):
        return tuple(_flash_step(q_ref[:, head(H_MOBA + n)], k_ref[rows_of(j), head(H_MOBA + n)],
                                 v_ref[rows_of(j), head(H_MOBA + n)], bias_ref[j], states[n])
                     for n in range(H_DSA))

    states = lax.fori_loop(0, i + 1, dsa_block, _flash_init(tq, H_DSA))
    for n in range(H_DSA):
        _, l, acc = states[n]
        o_ref[:, head(H_MOBA + n)] = (acc / l).astype(o_ref.dtype)

    u_tri2 = _tri2(kb)
    strict = col < row

    def sb_block(j, states, mask=None):
        out = []
        for n in range(H_SB):
            carry, acc = states[n]
            hs = head(H_MOBA + H_DSA + n)
            z = _dot_nt(q_ref[:, hs], k_ref[rows_of(j), hs])
            lf = -_softplus(z)
            if mask is not None:
                lf = jnp.where(mask, lf, 0.0)
            later = _suffix_sums(lf, u_tri2) + carry
            w = jnp.exp(z + lf + later)
            if mask is not None:
                w = jnp.where(mask, w, 0.0)
            acc = acc + _dot(w.astype(bf16), v_ref[rows_of(j), hs])
            out.append((carry + jnp.sum(lf, axis=-1, keepdims=True), acc))
        return tuple(out)

    states = tuple((jnp.zeros((tq, 1), f32), jnp.zeros((tq, HEAD_DIM), f32)) for _ in range(H_SB))
    states = sb_block(i, states, mask=strict)
    states = lax.fori_loop(0, i, lambda step, st: sb_block(i - 1 - step, st), states)
    for n in range(H_SB):
        o_ref[:, head(H_MOBA + H_DSA + n)] = states[n][1].astype(o_ref.dtype)


def _attn_prompt(q16, k16, v16, qi16, kk16, kw32, batch, t, n_keep):
    tq = MOBA_BLOCK
    nt = t // tq
    assert n_keep <= tq
    kern = functools.partial(_attn_prompt_kernel, n_keep=n_keep)
    return pl.pallas_call(
        kern,
        grid=(batch, nt),
        in_specs=[pl.BlockSpec((tq, MIX_W), lambda b, i: (b * nt + i, 0)),
                  pl.BlockSpec((t, MIX_W), lambda b, i: (b, 0)),
                  pl.BlockSpec((t, MIX_W), lambda b, i: (b, 0)),
                  pl.BlockSpec((tq, IDX_W), lambda b, i: (b * nt + i, 0)),
                  pl.BlockSpec((t, LANES), lambda b, i: (b, 0)),
                  pl.BlockSpec((tq, LANES), lambda b, i: (b * nt + i, 0))],
        out_specs=pl.BlockSpec((tq, MIX_W), lambda b, i: (b * nt + i, 0)),
        out_shape=jax.ShapeDtypeStruct((batch * t, MIX_W), bf16),
        scratch_shapes=[pltpu.VMEM((LANES, H_MOBA * HEAD_DIM), f32),
                        pltpu.VMEM((nt, tq, tq), jnp.int32),
                        pltpu.VMEM((nt, tq, tq), f32)],
        compiler_params=_cparams(("parallel", "arbitrary")),
        name="attn_prompt",
    )(q16, k16, v16, qi16, kk16, kw32)


def _samp_idx_kernel(pt_ref, qi_ref, wi_ref, kinew_ref, *refs, n_pg, n_keep):
    kidx_refs = refs[:n_pg]
    mask_ref, selfsel_ref, score_ref = refs[n_pg:]
    s = pl.program_id(1)
    qi = qi_ref[...]
    wi = wi_ref[...]
    for g in range(n_pg):
        kp = kidx_refs[g][...].astype(bf16)
        rel = jnp.maximum(_dot_nt(qi, kp), 0.0) * (IDX_DIM ** -0.5)
        score_ref[pl.ds(s * n_pg + g, 1), :] = jnp.sum(wi * rel, axis=0, keepdims=True)

    @pl.when(s == pl.num_programs(1) - 1)
    def _():
        sc = score_ref[...]
        n_pages, page = sc.shape
        ki_new = kinew_ref[...].astype(bf16).astype(f32)
        rel_self = jnp.maximum(jnp.sum(qi.astype(f32) * ki_new, axis=-1, keepdims=True), 0.0) * (IDX_DIM ** -0.5)
        sc_self = jnp.sum(wi[:, :1] * rel_self, axis=0, keepdims=True)
        skey = _sortable_key(sc)
        skey_self = _sortable_key(sc_self)

        def count(pred):
            return jnp.sum(jnp.sum(jnp.where(pred, 1.0, 0.0), axis=1, keepdims=True), axis=0, keepdims=True)

        def vbit(it, thr):
            cand = thr | lax.shift_left(jnp.int32(1), jnp.int32(31) - it)
            cs = cand ^ jnp.int32(INT_MIN)
            n_ge = count(skey >= cs) + jnp.where(skey_self >= cs, 1.0, 0.0)
            return jnp.where(n_ge >= n_keep, cand, thr)

        thr = lax.fori_loop(0, 32, vbit, jnp.zeros((1, 1), jnp.int32)) ^ jnp.int32(INT_MIN)
        gt = skey > thr
        tie = skey == thr
        gt_self = skey_self > thr
        tie_self = skey_self == thr
        need = float(n_keep) - count(gt) - jnp.where(gt_self, 1.0, 0.0)
        idx = (lax.broadcasted_iota(jnp.int32, sc.shape, 0) * page
               + lax.broadcasted_iota(jnp.int32, sc.shape, 1))
        n_bits = max(1, (n_pages * page - 1).bit_length())

        def ibit(it, c):
            cand = c | lax.shift_left(jnp.int32(1), jnp.int32(n_bits - 1) - it)
            return jnp.where(count(tie & (idx < cand)) < need, cand, c)

        cut = lax.fori_loop(0, n_bits, ibit, jnp.zeros((1, 1), jnp.int32))
        tie_taken = tie & (idx <= cut)
        mask_ref[...] = jnp.where(gt | tie_taken, 1.0, 0.0)
        self_taken = gt_self | (tie_self & (count(tie) < need))
        selfsel_ref[...] = jnp.broadcast_to(jnp.where(self_taken, 1.0, 0.0), selfsel_ref.shape)


def _samp_idx(page_table_flat, qi, wi, ki_new, cache_kidx, layer, n_pages, n_keep):
    bd = qi.shape[0]
    page = cache_kidx.shape[2]
    n_pg = min(16, n_pages)
    steps = n_pages // n_pg

    def kidx_spec(g):
        return pl.BlockSpec((None, None, page, IDX_DIM),
                            lambda b, s, pt: (pt[b * n_pages + s * n_pg + g], layer, 0, 0))

    kern = functools.partial(_samp_idx_kernel, n_pg=n_pg, n_keep=n_keep)
    grid_spec = pltpu.PrefetchScalarGridSpec(
        num_scalar_prefetch=1,
        grid=(bd, steps),
        in_specs=[pl.BlockSpec((None, N_IDX_HEADS, IDX_DIM), lambda b, s, pt: (b, 0, 0)),
                  pl.BlockSpec((None, N_IDX_HEADS, LANES), lambda b, s, pt: (b, 0, 0)),
                  pl.BlockSpec((None, 1, IDX_DIM), lambda b, s, pt: (b, 0, 0))]
                 + [kidx_spec(g) for g in range(n_pg)],
        out_specs=[pl.BlockSpec((None, n_pages, page), lambda b, s, pt: (b, 0, 0)),
                   pl.BlockSpec((None, 1, LANES), lambda b, s, pt: (b, 0, 0))],
        scratch_shapes=[pltpu.VMEM((n_pages, page), f32)],
    )
    return pl.pallas_call(
        kern,
        grid_spec=grid_spec,
        out_shape=[jax.ShapeDtypeStruct((bd, n_pages, page), f32),
                   jax.ShapeDtypeStruct((bd, 1, LANES), f32)],
        compiler_params=_cparams(("parallel", "arbitrary")),
        name="samp_idx",
    )(page_table_flat, qi, wi, ki_new, *([cache_kidx] * n_pg))


def _head_rows(page_ref, h, page):
    return page_ref[pl.ds(h, page, stride=N_HEADS), :].astype(bf16)


def _samp_attn_kernel(pt_ref, q_ref, knew_ref, vnew_ref, mask_ref, selfsel_ref, *refs, n_pg):
    k_refs = refs[:n_pg]
    v_refs = refs[n_pg:2 * n_pg]
    o_ref, m_ref, l_ref, acc_ref, carry_ref, pm_ref, pl_ref, pg_ref, pacc_ref = refs[2 * n_pg:]
    s = pl.program_id(1)
    n_pages, page = mask_ref.shape
    shape = (N_HEADS, page)
    row = lax.broadcasted_iota(jnp.int32, shape, 0)
    is_dsa = (row >= H_MOBA) & (row < H_MOBA + H_DSA)
    is_sb = row >= H_MOBA + H_DSA
    u_tri2 = _tri2(page)

    @pl.when(s == 0)
    def _():
        m_ref[...] = jnp.full(shape, NEG_INF, f32)
        l_ref[...] = jnp.zeros(shape, f32)
        acc_ref[...] = jnp.zeros((N_HEADS, HEAD_DIM), f32)
        carry_ref[...] = jnp.zeros(shape, f32)

    row_d = lax.broadcasted_iota(jnp.int32, (N_HEADS, HEAD_DIM), 0)
    for g in range(n_pg):
        pg = n_pages - 1 - (s * n_pg + g)
        sraw = jnp.zeros(shape, f32)
        for h in range(N_HEADS):
            qh = jnp.broadcast_to(q_ref[h:h + 1, :], (N_HEADS, HEAD_DIM))
            sraw = jnp.where(row == h, _dot_nt(qh, _head_rows(k_refs[g], h, page)), sraw)
        z = sraw
        mp = jnp.max(z, axis=-1, keepdims=True)
        e_a = jnp.exp(z - mp)
        dsel = mask_ref[pl.ds(pg, 1), :] > 0.5
        zd = jnp.where(dsel, z, NEG_INF)
        m_old = m_ref[...]
        m_new = jnp.maximum(m_old, jnp.max(zd, axis=-1, keepdims=True))
        m_safe = jnp.where(m_new == NEG_INF, 0.0, m_new)
        e_d = jnp.exp(zd - m_safe)
        alpha = jnp.where(is_dsa, jnp.exp(m_old - m_safe), 1.0)
        lf = -_softplus(z)
        later = _suffix_sums(lf, u_tri2) + carry_ref[...]
        w = jnp.exp(z + lf + later)
        p = jnp.where(is_sb, w, jnp.where(is_dsa, e_d, e_a))
        pb = p.astype(bf16)
        pv = jnp.zeros((N_HEADS, HEAD_DIM), f32)
        for h in range(N_HEADS):
            ph = jnp.broadcast_to(pb[h:h + 1, :], shape)
            pv = jnp.where(row_d == h, _dot(ph, _head_rows(v_refs[g], h, page)), pv)
        m_ref[...] = m_new
        l_ref[...] = alpha * l_ref[...] + jnp.sum(e_d, axis=-1, keepdims=True)
        acc_ref[...] = alpha * acc_ref[...] + pv
        carry_ref[...] = carry_ref[...] + jnp.sum(lf, axis=-1, keepdims=True)
        pm_ref[pg] = jnp.broadcast_to(mp, shape)
        pl_ref[pg] = jnp.broadcast_to(jnp.sum(e_a, axis=-1, keepdims=True), shape)
        pg_ref[pg] = jnp.broadcast_to(jnp.sum(sraw, axis=-1, keepdims=True), shape)
        pacc_ref[pg] = pv

    @pl.when(s == pl.num_programs(1) - 1)
    def _():
        k_new = knew_ref[...].astype(bf16).astype(f32)
        v_new = vnew_ref[...].astype(bf16).astype(f32)
        z_self = jnp.sum(q_ref[...].astype(f32) * k_new, axis=-1, keepdims=True)
        z_self = jnp.broadcast_to(z_self, shape)
        ppb = MOBA_BLOCK // page
        nb = n_pages // ppb
        gate = jnp.sum(pg_ref[...].reshape(nb, ppb, N_HEADS, page), axis=1) * (1.0 / MOBA_BLOCK)
        blk = lax.broadcasted_iota(jnp.int32, gate.shape, 0)
        sel = jnp.zeros(gate.shape, f32)
        for _ in range(min(MOBA_TOPK, nb)):
            best = jnp.max(gate, axis=0, keepdims=True)
            first = jnp.min(jnp.where(gate == best, blk, nb), axis=0, keepdims=True)
            pick = blk == first
            sel = jnp.where(pick, 1.0, sel)
            gate = jnp.where(pick, NEG_INF, gate)
        selp = jnp.broadcast_to(sel[:, None], (nb, ppb, N_HEADS, page)).reshape(n_pages, N_HEADS, page) > 0.5
        pm = pm_ref[...]
        m_a = jnp.maximum(jnp.max(jnp.where(selp, pm, NEG_INF), axis=0), z_self)
        wts = jnp.where(selp, jnp.exp(pm - m_a), 0.0)
        e_self = jnp.exp(z_self - m_a)
        l_a = jnp.sum(wts * pl_ref[...], axis=0) + e_self
        o_a = (jnp.sum(wts * pacc_ref[...], axis=0) + e_self * v_new) / l_a
        take = jnp.broadcast_to(selfsel_ref[...] > 0.5, shape)
        m_old = m_ref[...]
        m_d = jnp.where(take, jnp.maximum(m_old, z_self), m_old)
        m_safe = jnp.where(m_d == NEG_INF, 0.0, m_d)
        a_d = jnp.exp(m_old - m_safe)
        e_self_d = jnp.where(take, jnp.exp(z_self - m_safe), 0.0)
        o_d = (a_d * acc_ref[...] + e_self_d * v_new) / (a_d * l_ref[...] + e_self_d)
        o_ref[...] = jnp.where(is_sb, acc_ref[...], jnp.where(is_dsa, o_d, o_a))


def _samp_attn(page_table_flat, q16, k_new, v_new, mask, selfsel, cache_k, cache_v, layer):
    bd = q16.shape[0]
    n_pages, page = mask.shape[1:]
    n_pg = min(8, n_pages)
    steps = n_pages // n_pg

    def page_spec(g):
        return pl.BlockSpec((None, None, page * N_HEADS, HEAD_DIM),
                            lambda b, s, pt: (pt[b * n_pages + n_pages - 1 - (s * n_pg + g)], layer, 0, 0))

    def per_seq(shape):
        return pl.BlockSpec((None,) + shape, lambda b, s, pt: (b,) + (0,) * len(shape))

    kern = functools.partial(_samp_attn_kernel, n_pg=n_pg)
    stat = pltpu.VMEM((N_HEADS, page), f32)
    page_stat = pltpu.VMEM((n_pages, N_HEADS, page), f32)
    grid_spec = pltpu.PrefetchScalarGridSpec(
        num_scalar_prefetch=1,
        grid=(bd, steps),
        in_specs=[per_seq((N_HEADS, HEAD_DIM)), per_seq((N_HEADS, HEAD_DIM)),
                  per_seq((N_HEADS, HEAD_DIM)), per_seq((n_pages, page)), per_seq((1, LANES))]
                 + [page_spec(g) for g in range(n_pg)] * 2,
        out_specs=per_seq((N_HEADS, HEAD_DIM)),
        scratch_shapes=[stat, stat, pltpu.VMEM((N_HEADS, HEAD_DIM), f32), stat,
                        page_stat, page_stat, page_stat, page_stat],
    )
    return pl.pallas_call(
        kern,
        grid_spec=grid_spec,
        out_shape=jax.ShapeDtypeStruct((bd, N_HEADS, HEAD_DIM), f32),
        compiler_params=_cparams(("parallel", "arbitrary")),
        name="samp_attn",
    )(page_table_flat, q16, k_new, v_new, mask, selfsel, *([cache_k] * n_pg), *([cache_v] * n_pg))


def _rope_tables(pos, rot_dim, period, valid_lanes):
    half = rot_dim // 2
    inv_freq = jnp.power(ROPE_THETA, -jnp.arange(half, dtype=f32) * (2.0 / rot_dim))
    ang = pos.astype(f32)[:, None] * inv_freq[None, :]
    cos, sin = jnp.cos(ang), jnp.sin(ang)
    lane = jnp.arange(LANES)
    in_head = lane % period
    live = lane < valid_lanes
    first = (in_head < half) & live
    second = (in_head >= half) & (in_head < rot_dim) & live
    fidx = jnp.clip(in_head - jnp.where(second, half, 0), 0, half - 1)
    c = jnp.where((first | second)[None, :], cos[:, fidx], 1.0)
    a = jnp.where(first[None, :], -sin[:, fidx], 0.0)
    b = jnp.where(second[None, :], sin[:, fidx], 0.0)
    return c, a, b


def _head_vectors(g):
    gain = jnp.concatenate([jnp.tile(g[0], H_MOBA), jnp.tile(g[1], H_DSA), jnp.ones((H_SB * HEAD_DIM,), f32)])
    flag = jnp.concatenate([jnp.ones(((H_MOBA + H_DSA) * HEAD_DIM,), f32), jnp.zeros((H_SB * HEAD_DIM,), f32)])
    return gain[None, :], flag[None, :]


def _layer_weights(w_in16, w_out_l, l, d):
    o3 = 3 * MIX_W
    o5 = o3 + IDX_W + IDX_DIM + N_IDX_HEADS
    return dict(
        widx=jnp.pad(w_in16[l, :, o3:o5], ((0, 0), (0, IDX_PAD_W - (o5 - o3)))),
        wg=w_in16[l, :, o5:].reshape(d, N_BRANCH, d).transpose(1, 0, 2),
        wo=w_out_l.astype(bf16),
    )


def _dense_front(x, mod, ffn_w, l, norm_g_l, tm, rpg):
    h = _modnorm(x, norm_g_l[0:1], mod[0], mod[1], tm, rpg)
    x = _ffn(h, x, mod[2], *ffn_w, l, 0, tm, rpg)
    return x, _modnorm(x, norm_g_l[1:2], mod[3], mod[4], tm, rpg)


def _dense_back(x, h, o, mod, lw, ffn_w, l, norm_g_l, tm, rpg):
    x = _mixout(h, o, x, mod[5], lw["wg"], lw["wo"], tm, rpg)
    h = _modnorm(x, norm_g_l[2:3], mod[6], mod[7], tm, rpg)
    return _ffn(h, x, mod[8], *ffn_w, l, 1, tm, rpg)


def _projections(h, w_in16, l, lw, qv, kv, head_tables, idx_tables, tm, cache_dims=None, k_prev=None, v_prev=None):
    (q16,) = _proj_qk(h, w_in16, l, 0, qv[0], qv[1], head_tables, tm, SCALE, None)
    k16, k32 = _proj_qk(h, w_in16, l, MIX_W, kv[0], kv[1], head_tables, tm, 1.0,
                        "cache" if cache_dims else "plain", cache_dims, k_prev)
    v16, v32 = _proj_v(h, w_in16, l, 2 * MIX_W, tm, cache_dims, v_prev)
    qi16, kw32, kk16 = _proj_idx(h, lw["widx"], idx_tables, tm)
    return q16, k16, k32, v16, v32, qi16, kw32, kk16


def kernel(x_prompt, x_sample, cache_k, cache_v, cache_kidx, page_table, c_prompt, c_sample,
           w_ada, b_ada, norm_g, w_in, q_norm_g, k_norm_g, w_out, ffn_w1, ffn_w3, ffn_w2):
    batch, t, d = x_prompt.shape
    bd, dec_seq, _ = x_sample.shape
    assert dec_seq == 1 and bd <= SAMPLE_ROWS
    depth = w_ada.shape[0]
    n_phys, _, page, _, _ = cache_k.shape
    n_pages = page_table.shape[1]
    past_len = n_pages * page
    assert t % MOBA_BLOCK == 0 and past_len % MOBA_BLOCK == 0 and MOBA_BLOCK % page == 0
    keep_p = min(DSA_TOPK, t // 4)
    keep_s = min(DSA_TOPK, (past_len + dec_seq) // 4)
    m_p = batch * t
    tm_p = min(512, t)

    c_rows = batch + SAMPLE_ROWS
    c_all = jnp.zeros((-(-c_rows // 8) * 8, d), f32).at[:batch].set(c_prompt).at[batch:batch + bd].set(c_sample)
    mod_all = _ada(c_all, w_ada, b_ada)

    pos_p = jnp.arange(t, dtype=jnp.int32)
    pos_s = jnp.full((SAMPLE_ROWS,), past_len, jnp.int32)
    head_tab_p = _rope_tables(pos_p, ROPE_DIM, HEAD_DIM, LANES)
    head_tab_s = _rope_tables(pos_s, ROPE_DIM, HEAD_DIM, LANES)
    idx_tab_p = _rope_tables(pos_p, IDX_ROPE_DIM, IDX_DIM, LANES) + _rope_tables(pos_p, IDX_ROPE_DIM, IDX_DIM, IDX_DIM)
    idx_tab_s = _rope_tables(pos_s, IDX_ROPE_DIM, IDX_DIM, LANES) + _rope_tables(pos_s, IDX_ROPE_DIM, IDX_DIM, IDX_DIM)

    cache_k2 = cache_k.reshape(n_phys, depth, page * N_HEADS, HEAD_DIM)
    cache_v2 = cache_v.reshape(n_phys, depth, page * N_HEADS, HEAD_DIM)
    pt_flat = page_table.reshape(-1).astype(jnp.int32)

    w_in16 = w_in.astype(bf16)
    ffn_w = (ffn_w1.astype(bf16), ffn_w3.astype(bf16), ffn_w2.astype(bf16))
    xp = x_prompt.reshape(m_p, d)
    xs = jnp.zeros((SAMPLE_ROWS, d), f32).at[:bd].set(x_sample.reshape(bd, d))
    outs = [[] for _ in range(6)]
    k_all = v_all = None
    cache_dims = (batch, depth, t)
    for l in range(depth):
        lw = _layer_weights(w_in16, w_out[l], l, d)
        qv = _head_vectors(q_norm_g[l])
        kv = _head_vectors(k_norm_g[l])
        mod_l = mod_all[l].reshape(-1, N_MOD, d)
        mod_p = [mod_l[:batch, j].reshape(batch, 1, d) for j in range(N_MOD)]
        mod_s = [mod_l[batch:batch + SAMPLE_ROWS, j].reshape(1, SAMPLE_ROWS, d) for j in range(N_MOD)]

        xp, h = _dense_front(xp, mod_p, ffn_w, l, norm_g[l], tm_p, t)
        q16, k16, k_all, v16, v_all, qi16, kw32, kk16 = _projections(h, w_in16, l, lw, qv, kv, head_tab_p, idx_tab_p,
                                                                     tm_p, cache_dims, k_all, v_all)
        o16 = _attn_prompt(q16, k16, v16, qi16, kk16, kw32, batch, t, keep_p)
        xp = _dense_back(xp, h, o16, mod_p, lw, ffn_w, l, norm_g[l], tm_p, t)
        outs[2].append(kw32[:, :IDX_DIM].reshape(batch, t, IDX_DIM))

        xs, h = _dense_front(xs, mod_s, ffn_w, l, norm_g[l], SAMPLE_ROWS, SAMPLE_ROWS)
        q16, k16, k32, v16, v32, qi16, kw32, kk16 = _projections(h, w_in16, l, lw, qv, kv, head_tab_s, idx_tab_s,
                                                                 SAMPLE_ROWS)
        q_h = q16[:bd].reshape(bd, N_HEADS, HEAD_DIM)
        k_new = k32[:bd].reshape(bd, N_HEADS, HEAD_DIM)
        v_new = v32[:bd].reshape(bd, N_HEADS, HEAD_DIM)
        ki_new = kw32[:bd, :IDX_DIM]
        wi = jnp.broadcast_to(kw32[:bd, IDX_DIM:IDX_DIM + N_IDX_HEADS, None], (bd, N_IDX_HEADS, LANES))
        mask, selfsel = _samp_idx(pt_flat, qi16[:bd].reshape(bd, N_IDX_HEADS, IDX_DIM), wi,
                                  ki_new.reshape(bd, 1, IDX_DIM), cache_kidx, l, n_pages, keep_s)
        o_s = _samp_attn(pt_flat, q_h, k_new, v_new, mask, selfsel, cache_k2, cache_v2, l)
        o16 = jnp.zeros((SAMPLE_ROWS, MIX_W), bf16).at[:bd].set(o_s.reshape(bd, MIX_W).astype(bf16))
        xs = _dense_back(xs, h, o16, mod_s, lw, ffn_w, l, norm_g[l], SAMPLE_ROWS, SAMPLE_ROWS)
        outs[3].append(k_new.reshape(bd, 1, N_HEADS, HEAD_DIM))
        outs[4].append(v_new.reshape(bd, 1, N_HEADS, HEAD_DIM))
        outs[5].append(ki_new.reshape(bd, 1, IDX_DIM))

    stacked = [jnp.stack(o, axis=1) for o in outs[2:]]
    kv_shape = (batch, depth, t, N_HEADS, HEAD_DIM)
    return (xp.reshape(batch, t, d), xs[:bd].reshape(bd, 1, d), k_all.reshape(kv_shape), v_all.reshape(kv_shape),
            *stacked)
```

```python
import functools

import jax
import jax.numpy as jnp
from jax import lax
from jax.experimental import pallas as pl
from jax.experimental.pallas import tpu as pltpu

HEAD_DIM = 128
H_MOBA = 6
H_DSA = 4
H_SB = 6
N_HEADS = H_MOBA + H_DSA + H_SB
MIX_W = N_HEADS * HEAD_DIM
ROPE_DIM = HEAD_DIM // 4
ROPE_THETA = 500000.0
MOBA_BLOCK = 256
MOBA_TOPK = 3
N_IDX_HEADS = 16
IDX_DIM = 64
IDX_ROPE_DIM = IDX_DIM // 4
IDX_W = N_IDX_HEADS * IDX_DIM
IDX_PAD_W = IDX_W + 128
DSA_TOPK = 256
N_BRANCH = 3
N_MOD = 9
EPS = 1e-6
LANES = 128
SAMPLE_ROWS = 16
VMEM_LIMIT = 56 * 1024 * 1024
NEG_INF = float("-inf")
INT_MIN = -(2 ** 31)
SCALE = HEAD_DIM ** -0.5

f32 = jnp.float32
bf16 = jnp.bfloat16


def _cparams(sem):
    return pltpu.CompilerParams(dimension_semantics=sem, vmem_limit_bytes=VMEM_LIMIT)


def _dot(a, b):
    return jnp.dot(a, b, preferred_element_type=f32)


def _dot_nt(a, b):
    return lax.dot_general(a, b, (((1,), (1,)), ((), ())), preferred_element_type=f32)


def _split_bf16(x):
    hi = x.astype(bf16)
    lo = (x - hi.astype(f32)).astype(bf16)
    return hi, lo


def _softplus(z):
    return jnp.maximum(z, 0.0) + jnp.log(1.0 + jnp.exp(-jnp.abs(z)))


def _suffix_sums(lf, u_tri2):
    lf_hi, lf_lo = _split_bf16(lf)
    return _dot(jnp.concatenate([lf_hi, lf_lo], axis=1), u_tri2)


def _tri2(n):
    row = lax.broadcasted_iota(jnp.int32, (2 * n, n), 0)
    col = lax.broadcasted_iota(jnp.int32, (2 * n, n), 1)
    return jnp.where((row % n) > col, 1.0, 0.0).astype(bf16)


def _sortable_key(x):
    bits = pltpu.bitcast(x + 0.0, jnp.int32)
    return bits ^ ((bits >> 31) & jnp.int32(0x7FFFFFFF))


def _ada_kernel(c_ref, w_ref, b_ref, o_ref):
    c = c_ref[...]
    a = c * jax.nn.sigmoid(c)
    a_hi, a_lo = _split_bf16(a)
    w_hi, w_lo = _split_bf16(w_ref[...])
    o_ref[...] = _dot(a_hi, w_hi) + _dot(a_lo, w_hi) + _dot(a_hi, w_lo) + b_ref[...]


def _ada(c_all, w_ada, b_ada):
    depth, d, n = w_ada.shape
    rows = c_all.shape[0]
    tn = d // 2 if d >= 2 * LANES else d
    return pl.pallas_call(
        _ada_kernel,
        grid=(depth, n // tn),
        in_specs=[pl.BlockSpec((rows, d), lambda l, j: (0, 0)),
                  pl.BlockSpec((None, d, tn), lambda l, j: (l, 0, j)),
                  pl.BlockSpec((None, 1, tn), lambda l, j: (l, 0, j))],
        out_specs=pl.BlockSpec((None, rows, tn), lambda l, j: (l, 0, j)),
        out_shape=jax.ShapeDtypeStruct((depth, rows, n), f32),
        compiler_params=_cparams(("parallel", "parallel")),
        name="ada",
    )(c_all, w_ada, b_ada.reshape(depth, 1, n))


def _modnorm_kernel(x_ref, g_ref, shift_ref, scale_ref, o_ref):
    x = x_ref[...]
    ms = jnp.mean(x * x, axis=-1, keepdims=True)
    y = x * lax.rsqrt(ms + EPS) * g_ref[...]
    o_ref[...] = (y * (1.0 + scale_ref[...]) + shift_ref[...]).astype(o_ref.dtype)


def _mod_spec(mod, tm, rows_per_group):
    _, r, d = mod.shape
    return pl.BlockSpec((None, r, d), lambda i, *_: ((i * tm) // rows_per_group, 0, 0))


def _modnorm(x, g, shift, scale, tm, rows_per_group):
    m, d = x.shape
    return pl.pallas_call(
        _modnorm_kernel,
        grid=(m // tm,),
        in_specs=[pl.BlockSpec((tm, d), lambda i: (i, 0)),
                  pl.BlockSpec((1, d), lambda i: (0, 0)),
                  _mod_spec(shift, tm, rows_per_group),
                  _mod_spec(scale, tm, rows_per_group)],
        out_specs=pl.BlockSpec((tm, d), lambda i: (i, 0)),
        out_shape=jax.ShapeDtypeStruct((m, d), bf16),
        compiler_params=_cparams(("parallel",)),
        name="modnorm",
    )(x, g, shift, scale)


def _ffn_kernel(x_ref, g_ref, shift_ref, scale_ref, gate_ref, w1_ref, w3_ref, w2_ref, o_ref, acc_ref, h_ref):
    f = pl.program_id(1)

    @pl.when(f == 0)
    def _():
        acc_ref[...] = jnp.zeros_like(acc_ref)
        _modnorm_kernel(x_ref, g_ref, shift_ref, scale_ref, h_ref)

    h = h_ref[...]
    a = _dot(h, w1_ref[...])
    b = _dot(h, w3_ref[...])
    u = (a * jax.nn.sigmoid(a) * b).astype(bf16)
    acc_ref[...] += _dot(u, w2_ref[...])

    @pl.when(f == pl.num_programs(1) - 1)
    def _():
        o_ref[...] = x_ref[...] + 0.5 * gate_ref[...] * acc_ref[...]


def _ffn(x, g, shift, scale, gate, w1, w3, w2, layer, which, tm, rows_per_group):
    m, d = x.shape
    dff = w1.shape[-1]
    tf = min(512, dff)
    return pl.pallas_call(
        _ffn_kernel,
        grid=(m // tm, dff // tf),
        in_specs=[pl.BlockSpec((tm, d), lambda i, f: (i, 0)),
                  pl.BlockSpec((1, d), lambda i, f: (0, 0)),
                  _mod_spec(shift, tm, rows_per_group),
                  _mod_spec(scale, tm, rows_per_group),
                  _mod_spec(gate, tm, rows_per_group),
                  pl.BlockSpec((None, None, d, tf), lambda i, f: (layer, which, 0, f)),
                  pl.BlockSpec((None, None, d, tf), lambda i, f: (layer, which, 0, f)),
                  pl.BlockSpec((None, None, tf, d), lambda i, f: (layer, which, f, 0))],
        out_specs=pl.BlockSpec((tm, d), lambda i, f: (i, 0)),
        out_shape=jax.ShapeDtypeStruct((m, d), f32),
        scratch_shapes=[pltpu.VMEM((tm, d), f32), pltpu.VMEM((tm, d), bf16)],
        compiler_params=_cparams(("parallel", "arbitrary")),
        name="ffn",
    )(x, g, shift, scale, gate, w1, w3, w2)


def _rope(x, c, a, b, shift):
    return x * c + pltpu.roll(x, LANES - shift, 1) * a + pltpu.roll(x, shift, 1) * b


def _proj_qk_kernel(h_ref, w_ref, gain_ref, flag_ref, c_ref, a_ref, b_ref, *rest, out_scale, cache_rows):
    o_refs = rest[-2:] if cache_rows or len(rest) == 2 else rest[-1:]
    acc = _dot(h_ref[...], w_ref[...])
    c, a, b = c_ref[...], a_ref[...], b_ref[...]
    for hh in range(acc.shape[1] // HEAD_DIM):
        sl = slice(hh * HEAD_DIM, (hh + 1) * HEAD_DIM)
        x = acc[:, sl]
        ms = jnp.mean(x * x, axis=-1, keepdims=True)
        xn = x * lax.rsqrt(ms + EPS) * gain_ref[:, sl]
        out = jnp.where(flag_ref[:, sl] > 0.0, _rope(xn, c, a, b, ROPE_DIM // 2), x)
        if out_scale != 1.0:
            out = out * out_scale
        o_refs[0][:, sl] = out.astype(bf16)
        if cache_rows:
            o_refs[1][pl.ds(hh, cache_rows, stride=N_HEADS), :] = out
        elif len(o_refs) == 2:
            o_refs[1][:, sl] = out


def _table_spec(table, tm):
    nblk = table.shape[0] // tm
    return pl.BlockSpec((tm, LANES), lambda i, *_: (i % nblk, 0))


def _cache_out(m, tm, layer, cache_dims, prev):
    batch, depth, t = cache_dims
    nt = t // tm
    shape = jax.ShapeDtypeStruct((batch, depth, t * N_HEADS, HEAD_DIM), f32)
    spec = pl.BlockSpec((None, None, tm * N_HEADS, HEAD_DIM), lambda i, j: (i // nt, layer, i % nt, 0))
    extra_in = [] if prev is None else [prev]
    extra_specs = [] if prev is None else [pl.BlockSpec(memory_space=pl.ANY)]
    return shape, spec, extra_in, extra_specs


def _proj_qk(h, w_in, layer, col0, gain, flag, tables, tm, out_scale, f32_out, cache_dims=None, prev=None):
    m, d = h.shape
    n = MIX_W
    tn = n
    assert col0 % tn == 0
    out_shape = [jax.ShapeDtypeStruct((m, n), bf16)]
    out_specs = [pl.BlockSpec((tm, tn), lambda i, j: (i, j))]
    extra_in, extra_specs, aliases = [], [], {}
    if f32_out == "plain":
        out_shape.append(jax.ShapeDtypeStruct((m, n), f32))
        out_specs.append(pl.BlockSpec((tm, tn), lambda i, j: (i, j)))
    elif f32_out == "cache":
        shape, spec, extra_in, extra_specs = _cache_out(m, tm, layer, cache_dims, prev)
        out_shape.append(shape)
        out_specs.append(spec)
        if extra_in:
            aliases = {7: 1}
    return pl.pallas_call(
        functools.partial(_proj_qk_kernel, out_scale=out_scale, cache_rows=tm if f32_out == "cache" else 0),
        grid=(m // tm, n // tn),
        in_specs=[pl.BlockSpec((tm, d), lambda i, j: (i, 0)),
                  pl.BlockSpec((None, d, tn), lambda i, j: (layer, 0, col0 // tn + j)),
                  pl.BlockSpec((1, tn), lambda i, j: (0, j)),
                  pl.BlockSpec((1, tn), lambda i, j: (0, j))]
                 + [_table_spec(t, tm) for t in tables] + extra_specs,
        out_specs=out_specs,
        out_shape=out_shape,
        input_output_aliases=aliases,
        compiler_params=_cparams(("parallel", "parallel")),
        name="proj_qk",
    )(h, w_in, gain, flag, *tables, *extra_in)


def _proj_v_kernel(h_ref, w_ref, *rest, cache_rows):
    o16_ref, o32_ref = rest[-2:]
    acc = _dot(h_ref[...], w_ref[...])
    o16_ref[...] = acc.astype(bf16)
    if cache_rows:
        for hh in range(N_HEADS):
            o32_ref[pl.ds(hh, cache_rows, stride=N_HEADS), :] = acc[:, hh * HEAD_DIM:(hh + 1) * HEAD_DIM]
    else:
        o32_ref[...] = acc


def _proj_v(h, w_in, layer, col0, tm, cache_dims=None, prev=None):
    m, d = h.shape
    n = MIX_W
    tn = n
    assert col0 % tn == 0
    extra_in, extra_specs, aliases = [], [], {}
    if cache_dims:
        shape, spec, extra_in, extra_specs = _cache_out(m, tm, layer, cache_dims, prev)
        if extra_in:
            aliases = {2: 1}
    else:
        shape = jax.ShapeDtypeStruct((m, n), f32)
        spec = pl.BlockSpec((tm, tn), lambda i, j: (i, j))
    return pl.pallas_call(
        functools.partial(_proj_v_kernel, cache_rows=tm if cache_dims else 0),
        grid=(m // tm, n // tn),
        in_specs=[pl.BlockSpec((tm, d), lambda i, j: (i, 0)),
                  pl.BlockSpec((None, d, tn), lambda i, j: (layer, 0, col0 // tn + j))] + extra_specs,
        out_specs=[pl.BlockSpec((tm, tn), lambda i, j: (i, j)), spec],
        out_shape=[jax.ShapeDtypeStruct((m, n), bf16), shape],
        input_output_aliases=aliases,
        compiler_params=_cparams(("parallel", "parallel")),
        name="proj_v",
    )(h, w_in, *extra_in)


def _proj_idx_kernel(h_ref, w_ref, ci_ref, ai_ref, bi_ref, ck_ref, ak_ref, bk_ref, qi_ref, kw_ref, kk_ref):
    acc = _dot(h_ref[...], w_ref[...])
    ci, ai, bi = ci_ref[...], ai_ref[...], bi_ref[...]
    half = IDX_ROPE_DIM // 2
    for g in range(IDX_W // LANES):
        sl = slice(g * LANES, (g + 1) * LANES)
        qi_ref[:, sl] = _rope(acc[:, sl], ci, ai, bi, half).astype(bf16)
    chunk = _rope(acc[:, IDX_W:IDX_W + LANES], ck_ref[...], ak_ref[...], bk_ref[...], half)
    lane = lax.broadcasted_iota(jnp.int32, chunk.shape, 1)
    wi_scaled = chunk * (N_IDX_HEADS ** -0.5)
    kw_ref[...] = jnp.where(lane < IDX_DIM, chunk, jnp.where(lane < IDX_DIM + N_IDX_HEADS, wi_scaled, 0.0))
    kk_ref[...] = jnp.where(lane < IDX_DIM, chunk, pltpu.roll(chunk, IDX_DIM, 1)).astype(bf16)


def _proj_idx(h, w, tables, tm):
    m, d = h.shape
    return pl.pallas_call(
        _proj_idx_kernel,
        grid=(m // tm,),
        in_specs=[pl.BlockSpec((tm, d), lambda i: (i, 0)),
                  pl.BlockSpec((d, IDX_PAD_W), lambda i: (0, 0))]
                 + [_table_spec(t, tm) for t in tables],
        out_specs=[pl.BlockSpec((tm, IDX_W), lambda i: (i, 0)),
                   pl.BlockSpec((tm, LANES), lambda i: (i, 0)),
                   pl.BlockSpec((tm, LANES), lambda i: (i, 0))],
        out_shape=[jax.ShapeDtypeStruct((m, IDX_W), bf16),
                   jax.ShapeDtypeStruct((m, LANES), f32),
                   jax.ShapeDtypeStruct((m, LANES), bf16)],
        compiler_params=_cparams(("parallel",)),
        name="proj_idx",
    )(h, w, *tables)


_BRANCH_ROWS = ((0, H_MOBA * HEAD_DIM),
                (H_MOBA * HEAD_DIM, (H_MOBA + H_DSA) * HEAD_DIM),
                ((H_MOBA + H_DSA) * HEAD_DIM, MIX_W))


def _mixout_kernel(h_ref, o_ref, x_ref, gate_ref, wg0_ref, wg1_ref, wg2_ref, wo_ref, out_ref):
    h = h_ref[...]
    y = None
    for (r0, r1), wg_ref in zip(_BRANCH_ROWS, (wg0_ref, wg1_ref, wg2_ref)):
        g = jax.nn.sigmoid(_dot(h, wg_ref[...]))
        term = g * _dot(o_ref[:, r0:r1], wo_ref[r0:r1, :])
        y = term if y is None else y + term
    out_ref[...] = x_ref[...] + gate_ref[...] * y


def _mixout(h, o, x, gate, wg, wo, tm, rows_per_group):
    m, d = x.shape
    tn = min(512, d)

    def gate_spec(b):
        return pl.BlockSpec((d, tn), lambda i, j: (0, b * (d // tn) + j))

    r = gate.shape[1]
    return pl.pallas_call(
        _mixout_kernel,
        grid=(m // tm, d // tn),
        in_specs=[pl.BlockSpec((tm, d), lambda i, j: (i, 0)),
                  pl.BlockSpec((tm, MIX_W), lambda i, j: (i, 0)),
                  pl.BlockSpec((tm, tn), lambda i, j: (i, j)),
                  pl.BlockSpec((None, r, tn), lambda i, j: ((i * tm) // rows_per_group, 0, j)),
                  gate_spec(0), gate_spec(1), gate_spec(2),
                  pl.BlockSpec((MIX_W, tn), lambda i, j: (0, j))],
        out_specs=pl.BlockSpec((tm, tn), lambda i, j: (i, j)),
        out_shape=jax.ShapeDtypeStruct((m, d), f32),
        compiler_params=_cparams(("parallel", "parallel")),
        name="mixout",
    )(h, o, x, gate, wg, wg, wg, wo)


BIAS_OFF = -1e30


def _flash_step(qh, kj, vj, bias, state):
    m, l, acc = state
    s = _dot_nt(qh, kj) + bias
    m_new = jnp.maximum(m, jnp.max(s, axis=-1, keepdims=True))
    alpha = jnp.exp(m - m_new)
    p = jnp.exp(s - m_new)
    l = alpha * l + jnp.sum(p, axis=-1, keepdims=True)
    acc = alpha * acc + _dot(p.astype(bf16), vj)
    return m_new, l, acc


def _flash_init(tq, n):
    return tuple((jnp.full((tq, 1), BIAS_OFF, f32), jnp.zeros((tq, 1), f32), jnp.zeros((tq, HEAD_DIM), f32))
                 for _ in range(n))


def _attn_prompt_kernel(q_ref, k_ref, v_ref, qi_ref, kk_ref, kwq_ref, o_ref, kmean_ref, skey_ref, bias_ref, *,
                        n_keep):
    tq = q_ref.shape[0]
    t = k_ref.shape[0]
    kb = tq
    i = pl.program_id(1)
    row = lax.broadcasted_iota(jnp.int32, (tq, kb), 0)
    col = lax.broadcasted_iota(jnp.int32, (tq, kb), 1)
    diag_bias = jnp.where(col <= row, 0.0, BIAS_OFF)

    def head(h):
        return slice(h * HEAD_DIM, (h + 1) * HEAD_DIM)

    def rows_of(j):
        return pl.ds(pl.multiple_of(j * kb, kb), kb)

    @pl.when(i == 0)
    def _():
        e_row = lax.broadcasted_iota(jnp.int32, (LANES, t), 0)
        e_col = lax.broadcasted_iota(jnp.int32, (LANES, t), 1)
        e_ind = jnp.where(e_col // MOBA_BLOCK == e_row, 1.0, 0.0).astype(bf16)
        kmean_ref[...] = _dot(e_ind, k_ref[:, :H_MOBA * HEAD_DIM]) * (1.0 / MOBA_BLOCK)

    blk = lax.broadcasted_iota(jnp.int32, (tq, LANES), 1)
    past = blk < i
    sel_past = []
    for h in range(H_MOBA):
        qh = q_ref[:, head(h)]
        km_hi, km_lo = _split_bf16(kmean_ref[:, head(h)])
        gate = _dot_nt(qh, km_hi) + _dot_nt(qh, km_lo)
        gate = jnp.where(past, gate, NEG_INF)
        rank = jnp.zeros((tq, LANES), f32)
        for mblk in range(t // MOBA_BLOCK):
            g_m = gate[:, mblk:mblk + 1]
            wins_tie = jnp.where(mblk < blk, 1.0, 0.0)
            rank = rank + jnp.where(g_m > gate, 1.0, jnp.where(g_m == gate, wins_tie, 0.0))
        sel_past.append(jnp.where(past, jnp.where(rank < MOBA_TOPK, 1.0, 0.0), 0.0))

    def moba_block(j, states):
        out = []
        for h in range(H_MOBA):
            picked = jnp.sum(jnp.where(blk == j, sel_past[h], 0.0), axis=-1, keepdims=True)
            bias = (picked - 1.0) * (-BIAS_OFF)
            out.append(_flash_step(q_ref[:, head(h)], k_ref[rows_of(j), head(h)], v_ref[rows_of(j), head(h)],
                                   bias, states[h]))
        return tuple(out)

    states = lax.fori_loop(0, i, moba_block, _flash_init(tq, H_MOBA))
    for h in range(H_MOBA):
        _, l, acc = _flash_step(q_ref[:, head(h)], k_ref[rows_of(i), head(h)], v_ref[rows_of(i), head(h)],
                                diag_bias, states[h])
        o_ref[:, head(h)] = (acc / l).astype(o_ref.dtype)

    lane = lax.broadcasted_iota(jnp.int32, (tq, LANES), 1)
    kwq = kwq_ref[...]
    q_idx, w_idx = [], []
    for g in range(IDX_W // LANES):
        chunk = qi_ref[:, g * LANES:(g + 1) * LANES]
        for half in range(LANES // IDX_DIM):
            hidx = g * (LANES // IDX_DIM) + half
            in_half = (lane >= half * IDX_DIM) & (lane < (half + 1) * IDX_DIM)
            q_idx.append(jnp.where(in_half, chunk, jnp.zeros_like(chunk)))
            w_idx.append(kwq[:, IDX_DIM + hidx:IDX_DIM + hidx + 1] * (IDX_DIM ** -0.5))

    def adm_of(j):
        return (j * kb + col) <= (i * kb + row)

    def idx_block(j, carry):
        kkj = kk_ref[rows_of(j), :]
        score = jnp.zeros((tq, kb), f32)
        for qm, wh in zip(q_idx, w_idx):
            score = score + jnp.maximum(_dot_nt(qm, kkj), 0.0) * wh
        skey_ref[j] = jnp.where(adm_of(j), _sortable_key(score), jnp.int32(INT_MIN))
        return carry

    lax.fori_loop(0, i + 1, idx_block, 0)

    nt = t // kb

    def blank(j, carry):
        skey_ref[j] = jnp.full((tq, kb), INT_MIN, jnp.int32)
        return carry

    lax.fori_loop(i + 1, nt, blank, 0)
    n_bits = max(1, (t - 1).bit_length())

    def search(nblk):
        def count(pred_of_block):
            c = jnp.where(pred_of_block(0), 1.0, 0.0)
            for jj in range(1, nblk):
                c = c + jnp.where(pred_of_block(jj), 1.0, 0.0)
            return jnp.sum(c, axis=-1, keepdims=True)

        def vbit(it, thr):
            cand = thr | lax.shift_left(jnp.int32(1), jnp.int32(31) - it)
            cand_s = cand ^ jnp.int32(INT_MIN)
            return jnp.where(count(lambda jj: skey_ref[jj] >= cand_s) >= n_keep, cand, thr)

        thr = lax.fori_loop(0, 32, vbit, jnp.zeros((tq, 1), jnp.int32)) ^ jnp.int32(INT_MIN)
        need = float(n_keep) - count(lambda jj: skey_ref[jj] > thr)
        n_tie = count(lambda jj: skey_ref[jj] == thr)

        def tie_cut():
            def ibit(it, c):
                cand = c | lax.shift_left(jnp.int32(1), jnp.int32(n_bits - 1) - it)
                below = count(lambda jj: (skey_ref[jj] == thr) & ((jj * kb + col) < cand))
                return jnp.where(below < need, cand, c)

            return lax.fori_loop(0, n_bits, ibit, jnp.zeros((tq, 1), jnp.int32))

        cut = lax.cond(jnp.max(n_tie - need) > 0.0, tie_cut, lambda: jnp.full((tq, 1), t, jnp.int32))
        return thr, cut

    group = max(1, nt // 4)
    widths = list(range(group, nt + 1, group))

    def pick(widths):
        if len(widths) == 1:
            return search(widths[0])
        return lax.cond(i < widths[0], lambda: search(widths[0]), lambda: pick(widths[1:]))

    thr, cut = pick(widths)

    def bias_block(j, carry):
        key = skey_ref[j]
        idx_ok = (j * kb + col) <= cut
        taken = jnp.where(key > thr, 0.0, jnp.where(key == thr, jnp.where(idx_ok, 0.0, BIAS_OFF), BIAS_OFF))
        bias_ref[j] = jnp.where(adm_of(j), taken, BIAS_OFF)
        return carry

    lax.fori_loop(0, i + 1, bias_block, 0)

    def dsa_block(j, states):
        return tuple(_flash_step(q_ref[:, head(H_MOBA + n)], k_ref[rows_of(j), head(H_MOBA + n)],
                                 v_ref[rows_of(j), head(H_MOBA + n)], bias_ref[j], states[n])
                     for n in range(H_DSA))

    states = lax.fori_loop(0, i + 1, dsa_block, _flash_init(tq, H_DSA))
    for n in range(H_DSA):
        _, l, acc = states[n]
        o_ref[:, head(H_MOBA + n)] = (acc / l).astype(o_ref.dtype)

    u_tri2 = _tri2(kb)
    strict = col < row

    def sb_block(j, states, mask=None):
        out = []
        for n in range(H_SB):
            carry, acc = states[n]
            hs = head(H_MOBA + H_DSA + n)
            z = _dot_nt(q_ref[:, hs], k_ref[rows_of(j), hs])
            lf = -_softplus(z)
            if mask is not None:
                lf = jnp.where(mask, lf, 0.0)
            later = _suffix_sums(lf, u_tri2) + carry
            w = jnp.exp(z + lf + later)
            if mask is not None:
                w = jnp.where(mask, w, 0.0)
            acc = acc + _dot(w.astype(bf16), v_ref[rows_of(j), hs])
            out.append((carry + jnp.sum(lf, axis=-1, keepdims=True), acc))
        return tuple(out)

    states = tuple((jnp.zeros((tq, 1), f32), jnp.zeros((tq, HEAD_DIM), f32)) for _ in range(H_SB))
    states = sb_block(i, states, mask=strict)
    states = lax.fori_loop(0, i, lambda step, st: sb_block(i - 1 - step, st), states)
    for n in range(H_SB):
        o_ref[:, head(H_MOBA + H_DSA + n)] = states[n][1].astype(o_ref.dtype)


def _attn_prompt(q16, k16, v16, qi16, kk16, kw32, batch, t, n_keep):
    tq = MOBA_BLOCK
    nt = t // tq
    assert n_keep <= tq
    kern = functools.partial(_attn_prompt_kernel, n_keep=n_keep)
    return pl.pallas_call(
        kern,
        grid=(batch, nt),
        in_specs=[pl.BlockSpec((tq, MIX_W), lambda b, i: (b * nt + i, 0)),
                  pl.BlockSpec((t, MIX_W), lambda b, i: (b, 0)),
                  pl.BlockSpec((t, MIX_W), lambda b, i: (b, 0)),
                  pl.BlockSpec((tq, IDX_W), lambda b, i: (b * nt + i, 0)),
                  pl.BlockSpec((t, LANES), lambda b, i: (b, 0)),
                  pl.BlockSpec((tq, LANES), lambda b, i: (b * nt + i, 0))],
        out_specs=pl.BlockSpec((tq, MIX_W), lambda b, i: (b * nt + i, 0)),
        out_shape=jax.ShapeDtypeStruct((batch * t, MIX_W), bf16),
        scratch_shapes=[pltpu.VMEM((LANES, H_MOBA * HEAD_DIM), f32),
                        pltpu.VMEM((nt, tq, tq), jnp.int32),
                        pltpu.VMEM((nt, tq, tq), f32)],
        compiler_params=_cparams(("parallel", "arbitrary")),
        name="attn_prompt",
    )(q16, k16, v16, qi16, kk16, kw32)


def _samp_idx_kernel(pt_ref, qi_ref, wi_ref, kinew_ref, *refs, n_pg, n_keep):
    kidx_refs = refs[:n_pg]
    mask_ref, selfsel_ref, score_ref = refs[n_pg:]
    s = pl.program_id(1)
    qi = qi_ref[...]
    wi = wi_ref[...]
    for g in range(n_pg):
        kp = kidx_refs[g][...].astype(bf16)
        rel = jnp.maximum(_dot_nt(qi, kp), 0.0) * (IDX_DIM ** -0.5)
        score_ref[pl.ds(s * n_pg + g, 1), :] = jnp.sum(wi * rel, axis=0, keepdims=True)

    @pl.when(s == pl.num_programs(1) - 1)
    def _():
        sc = score_ref[...]
        n_pages, page = sc.shape
        ki_new = kinew_ref[...].astype(bf16).astype(f32)
        rel_self = jnp.maximum(jnp.sum(qi.astype(f32) * ki_new, axis=-1, keepdims=True), 0.0) * (IDX_DIM ** -0.5)
        sc_self = jnp.sum(wi[:, :1] * rel_self, axis=0, keepdims=True)
        skey = _sortable_key(sc)
        skey_self = _sortable_key(sc_self)

        def count(pred):
            return jnp.sum(jnp.sum(jnp.where(pred, 1.0, 0.0), axis=1, keepdims=True), axis=0, keepdims=True)

        def vbit(it, thr):
            cand = thr | lax.shift_left(jnp.int32(1), jnp.int32(31) - it)
            cs = cand ^ jnp.int32(INT_MIN)
            n_ge = count(skey >= cs) + jnp.where(skey_self >= cs, 1.0, 0.0)
            return jnp.where(n_ge >= n_keep, cand, thr)

        thr = lax.fori_loop(0, 32, vbit, jnp.zeros((1, 1), jnp.int32)) ^ jnp.int32(INT_MIN)
        gt = skey > thr
        tie = skey == thr
        gt_self = skey_self > thr
        tie_self = skey_self == thr
        need = float(n_keep) - count(gt) - jnp.where(gt_self, 1.0, 0.0)
        idx = (lax.broadcasted_iota(jnp.int32, sc.shape, 0) * page
               + lax.broadcasted_iota(jnp.int32, sc.shape, 1))
        n_bits = max(1, (n_pages * page - 1).bit_length())

        def ibit(it, c):
            cand = c | lax.shift_left(jnp.int32(1), jnp.int32(n_bits - 1) - it)
            return jnp.where(count(tie & (idx < cand)) < need, cand, c)

        cut = lax.fori_loop(0, n_bits, ibit, jnp.zeros((1, 1), jnp.int32))
        tie_taken = tie & (idx <= cut)
        mask_ref[...] = jnp.where(gt | tie_taken, 1.0, 0.0)
        self_taken = gt_self | (tie_self & (count(tie) < need))
        selfsel_ref[...] = jnp.broadcast_to(jnp.where(self_taken, 1.0, 0.0), selfsel_ref.shape)


def _samp_idx(page_table_flat, qi, wi, ki_new, cache_kidx, layer, n_pages, n_keep):
    bd = qi.shape[0]
    page = cache_kidx.shape[2]
    n_pg = min(16, n_pages)
    steps = n_pages // n_pg

    def kidx_spec(g):
        return pl.BlockSpec((None, None, page, IDX_DIM),
                            lambda b, s, pt: (pt[b * n_pages + s * n_pg + g], layer, 0, 0))

    kern = functools.partial(_samp_idx_kernel, n_pg=n_pg, n_keep=n_keep)
    grid_spec = pltpu.PrefetchScalarGridSpec(
        num_scalar_prefetch=1,
        grid=(bd, steps),
        in_specs=[pl.BlockSpec((None, N_IDX_HEADS, IDX_DIM), lambda b, s, pt: (b, 0, 0)),
                  pl.BlockSpec((None, N_IDX_HEADS, LANES), lambda b, s, pt: (b, 0, 0)),
                  pl.BlockSpec((None, 1, IDX_DIM), lambda b, s, pt: (b, 0, 0))]
                 + [kidx_spec(g) for g in range(n_pg)],
        out_specs=[pl.BlockSpec((None, n_pages, page), lambda b, s, pt: (b, 0, 0)),
                   pl.BlockSpec((None, 1, LANES), lambda b, s, pt: (b, 0, 0))],
        scratch_shapes=[pltpu.VMEM((n_pages, page), f32)],
    )
    return pl.pallas_call(
        kern,
        grid_spec=grid_spec,
        out_shape=[jax.ShapeDtypeStruct((bd, n_pages, page), f32),
                   jax.ShapeDtypeStruct((bd, 1, LANES), f32)],
        compiler_params=_cparams(("parallel", "arbitrary")),
        name="samp_idx",
    )(page_table_flat, qi, wi, ki_new, *([cache_kidx] * n_pg))


def _head_rows(page_ref, h, page):
    return page_ref[pl.ds(h, page, stride=N_HEADS), :].astype(bf16)


def _samp_attn_kernel(pt_ref, q_ref, knew_ref, vnew_ref, mask_ref, selfsel_ref, *refs, n_pg):
    k_refs = refs[:n_pg]
    v_refs = refs[n_pg:2 * n_pg]
    o_ref, m_ref, l_ref, acc_ref, carry_ref, pm_ref, pl_ref, pg_ref, pacc_ref = refs[2 * n_pg:]
    s = pl.program_id(1)
    n_pages, page = mask_ref.shape
    shape = (N_HEADS, page)
    row = lax.broadcasted_iota(jnp.int32, shape, 0)
    is_dsa = (row >= H_MOBA) & (row < H_MOBA + H_DSA)
    is_sb = row >= H_MOBA + H_DSA
    u_tri2 = _tri2(page)

    @pl.when(s == 0)
    def _():
        m_ref[...] = jnp.full(shape, NEG_INF, f32)
        l_ref[...] = jnp.zeros(shape, f32)
        acc_ref[...] = jnp.zeros((N_HEADS, HEAD_DIM), f32)
        carry_ref[...] = jnp.zeros(shape, f32)

    row_d = lax.broadcasted_iota(jnp.int32, (N_HEADS, HEAD_DIM), 0)
    for g in range(n_pg):
        pg = n_pages - 1 - (s * n_pg + g)
        sraw = jnp.zeros(shape, f32)
        for h in range(N_HEADS):
            qh = jnp.broadcast_to(q_ref[h:h + 1, :], (N_HEADS, HEAD_DIM))
            sraw = jnp.where(row == h, _dot_nt(qh, _head_rows(k_refs[g], h, page)), sraw)
        z = sraw
        mp = jnp.max(z, axis=-1, keepdims=True)
        e_a = jnp.exp(z - mp)
        dsel = mask_ref[pl.ds(pg, 1), :] > 0.5
        zd = jnp.where(dsel, z, NEG_INF)
        m_old = m_ref[...]
        m_new = jnp.maximum(m_old, jnp.max(zd, axis=-1, keepdims=True))
        m_safe = jnp.where(m_new == NEG_INF, 0.0, m_new)
        e_d = jnp.exp(zd - m_safe)
        alpha = jnp.where(is_dsa, jnp.exp(m_old - m_safe), 1.0)
        lf = -_softplus(z)
        later = _suffix_sums(lf, u_tri2) + carry_ref[...]
        w = jnp.exp(z + lf + later)
        p = jnp.where(is_sb, w, jnp.where(is_dsa, e_d, e_a))
        pb = p.astype(bf16)
        pv = jnp.zeros((N_HEADS, HEAD_DIM), f32)
        for h in range(N_HEADS):
            ph = jnp.broadcast_to(pb[h:h + 1, :], shape)
            pv = jnp.where(row_d == h, _dot(ph, _head_rows(v_refs[g], h, page)), pv)
        m_ref[...] = m_new
        l_ref[...] = alpha * l_ref[...] + jnp.sum(e_d, axis=-1, keepdims=True)
        acc_ref[...] = alpha * acc_ref[...] + pv
        carry_ref[...] = carry_ref[...] + jnp.sum(lf, axis=-1, keepdims=True)
        pm_ref[pg] = jnp.broadcast_to(mp, shape)
        pl_ref[pg] = jnp.broadcast_to(jnp.sum(e_a, axis=-1, keepdims=True), shape)
        pg_ref[pg] = jnp.broadcast_to(jnp.sum(sraw, axis=-1, keepdims=True), shape)
        pacc_ref[pg] = pv

    @pl.when(s == pl.num_programs(1) - 1)
    def _():
        k_new = knew_ref[...].astype(bf16).astype(f32)
        v_new = vnew_ref[...].astype(bf16).astype(f32)
        z_self = jnp.sum(q_ref[...].astype(f32) * k_new, axis=-1, keepdims=True)
        z_self = jnp.broadcast_to(z_self, shape)
        ppb = MOBA_BLOCK // page
        nb = n_pages // ppb
        gate = jnp.sum(pg_ref[...].reshape(nb, ppb, N_HEADS, page), axis=1) * (1.0 / MOBA_BLOCK)
        blk = lax.broadcasted_iota(jnp.int32, gate.shape, 0)
        sel = jnp.zeros(gate.shape, f32)
        for _ in range(min(MOBA_TOPK, nb)):
            best = jnp.max(gate, axis=0, keepdims=True)
            first = jnp.min(jnp.where(gate == best, blk, nb), axis=0, keepdims=True)
            pick = blk == first
            sel = jnp.where(pick, 1.0, sel)
            gate = jnp.where(pick, NEG_INF, gate)
        selp = jnp.broadcast_to(sel[:, None], (nb, ppb, N_HEADS, page)).reshape(n_pages, N_HEADS, page) > 0.5
        pm = pm_ref[...]
        m_a = jnp.maximum(jnp.max(jnp.where(selp, pm, NEG_INF), axis=0), z_self)
        wts = jnp.where(selp, jnp.exp(pm - m_a), 0.0)
        e_self = jnp.exp(z_self - m_a)
        l_a = jnp.sum(wts * pl_ref[...], axis=0) + e_self
        o_a = (jnp.sum(wts * pacc_ref[...], axis=0) + e_self * v_new) / l_a
        take = jnp.broadcast_to(selfsel_ref[...] > 0.5, shape)
        m_old = m_ref[...]
        m_d = jnp.where(take, jnp.maximum(m_old, z_self), m_old)
        m_safe = jnp.where(m_d == NEG_INF, 0.0, m_d)
        a_d = jnp.exp(m_old - m_safe)
        e_self_d = jnp.where(take, jnp.exp(z_self - m_safe), 0.0)
        o_d = (a_d * acc_ref[...] + e_self_d * v_new) / (a_d * l_ref[...] + e_self_d)
        o_ref[...] = jnp.where(is_sb, acc_ref[...], jnp.where(is_dsa, o_d, o_a))


def _samp_attn(page_table_flat, q16, k_new, v_new, mask, selfsel, cache_k, cache_v, layer):
    bd = q16.shape[0]
    n_pages, page = mask.shape[1:]
    n_pg = min(8, n_pages)
    steps = n_pages // n_pg

    def page_spec(g):
        return pl.BlockSpec((None, None, page * N_HEADS, HEAD_DIM),
                            lambda b, s, pt: (pt[b * n_pages + n_pages - 1 - (s * n_pg + g)], layer, 0, 0))

    def per_seq(shape):
        return pl.BlockSpec((None,) + shape, lambda b, s, pt: (b,) + (0,) * len(shape))

    kern = functools.partial(_samp_attn_kernel, n_pg=n_pg)
    stat = pltpu.VMEM((N_HEADS, page), f32)
    page_stat = pltpu.VMEM((n_pages, N_HEADS, page), f32)
    grid_spec = pltpu.PrefetchScalarGridSpec(
        num_scalar_prefetch=1,
        grid=(bd, steps),
        in_specs=[per_seq((N_HEADS, HEAD_DIM)), per_seq((N_HEADS, HEAD_DIM)),
                  per_seq((N_HEADS, HEAD_DIM)), per_seq((n_pages, page)), per_seq((1, LANES))]
                 + [page_spec(g) for g in range(n_pg)] * 2,
        out_specs=per_seq((N_HEADS, HEAD_DIM)),
        scratch_shapes=[stat, stat, pltpu.VMEM((N_HEADS, HEAD_DIM), f32), stat,
                        page_stat, page_stat, page_stat, page_stat],
    )
    return pl.pallas_call(
        kern,
        grid_spec=grid_spec,
        out_shape=jax.ShapeDtypeStruct((bd, N_HEADS, HEAD_DIM), f32),
        compiler_params=_cparams(("parallel", "arbitrary")),
        name="samp_attn",
    )(page_table_flat, q16, k_new, v_new, mask, selfsel, *([cache_k] * n_pg), *([cache_v] * n_pg))


def _rope_tables(pos, rot_dim, period, valid_lanes):
    half = rot_dim // 2
    inv_freq = jnp.power(ROPE_THETA, -jnp.arange(half, dtype=f32) * (2.0 / rot_dim))
    ang = pos.astype(f32)[:, None] * inv_freq[None, :]
    cos, sin = jnp.cos(ang), jnp.sin(ang)
    lane = jnp.arange(LANES)
    in_head = lane % period
    live = lane < valid_lanes
    first = (in_head < half) & live
    second = (in_head >= half) & (in_head < rot_dim) & live
    fidx = jnp.clip(in_head - jnp.where(second, half, 0), 0, half - 1)
    c = jnp.where((first | second)[None, :], cos[:, fidx], 1.0)
    a = jnp.where(first[None, :], -sin[:, fidx], 0.0)
    b = jnp.where(second[None, :], sin[:, fidx], 0.0)
    return c, a, b


def _head_vectors(g):
    gain = jnp.concatenate([jnp.tile(g[0], H_MOBA), jnp.tile(g[1], H_DSA), jnp.ones((H_SB * HEAD_DIM,), f32)])
    flag = jnp.concatenate([jnp.ones(((H_MOBA + H_DSA) * HEAD_DIM,), f32), jnp.zeros((H_SB * HEAD_DIM,), f32)])
    return gain[None, :], flag[None, :]


def _layer_weights(w_in16, w_out_l, l, d):
    o3 = 3 * MIX_W
    o5 = o3 + IDX_W + IDX_DIM + N_IDX_HEADS
    return dict(
        widx=jnp.pad(w_in16[l, :, o3:o5], ((0, 0), (0, IDX_PAD_W - (o5 - o3)))),
        wg=w_in16[l, :, o5:],
        wo=w_out_l.astype(bf16),
    )


def _dense_front(x, mod, ffn_w, l, norm_g_l, tm, rpg):
    x = _ffn(x, norm_g_l[0:1], mod[0], mod[1], mod[2], *ffn_w, l, 0, tm, rpg)
    return x, _modnorm(x, norm_g_l[1:2], mod[3], mod[4], tm, rpg)


def _dense_back(x, h, o, mod, lw, ffn_w, l, norm_g_l, tm, rpg):
    x = _mixout(h, o, x, mod[5], lw["wg"], lw["wo"], tm, rpg)
    return _ffn(x, norm_g_l[2:3], mod[6], mod[7], mod[8], *ffn_w, l, 1, tm, rpg)


def _projections(h, w_in16, l, lw, qv, kv, head_tables, idx_tables, tm, cache_dims=None, k_prev=None, v_prev=None):
    (q16,) = _proj_qk(h, w_in16, l, 0, qv[0], qv[1], head_tables, tm, SCALE, None)
    k16, k32 = _proj_qk(h, w_in16, l, MIX_W, kv[0], kv[1], head_tables, tm, 1.0,
                        "cache" if cache_dims else "plain", cache_dims, k_prev)
    v16, v32 = _proj_v(h, w_in16, l, 2 * MIX_W, tm, cache_dims, v_prev)
    qi16, kw32, kk16 = _proj_idx(h, lw["widx"], idx_tables, tm)
    return q16, k16, k32, v16, v32, qi16, kw32, kk16


def kernel(x_prompt, x_sample, cache_k, cache_v, cache_kidx, page_table, c_prompt, c_sample,
           w_ada, b_ada, norm_g, w_in, q_norm_g, k_norm_g, w_out, ffn_w1, ffn_w3, ffn_w2):
    batch, t, d = x_prompt.shape
    bd, dec_seq, _ = x_sample.shape
    assert dec_seq == 1 and bd <= SAMPLE_ROWS
    depth = w_ada.shape[0]
    n_phys, _, page, _, _ = cache_k.shape
    n_pages = page_table.shape[1]
    past_len = n_pages * page
    assert t % MOBA_BLOCK == 0 and past_len % MOBA_BLOCK == 0 and MOBA_BLOCK % page == 0
    keep_p = min(DSA_TOPK, t // 4)
    keep_s = min(DSA_TOPK, (past_len + dec_seq) // 4)
    m_p = batch * t
    tm_p = min(512, t)

    c_rows = batch + SAMPLE_ROWS
    c_all = jnp.zeros((-(-c_rows // 8) * 8, d), f32).at[:batch].set(c_prompt).at[batch:batch + bd].set(c_sample)
    mod_all = _ada(c_all, w_ada, b_ada)

    pos_p = jnp.arange(t, dtype=jnp.int32)
    pos_s = jnp.full((SAMPLE_ROWS,), past_len, jnp.int32)
    head_tab_p = _rope_tables(pos_p, ROPE_DIM, HEAD_DIM, LANES)
    head_tab_s = _rope_tables(pos_s, ROPE_DIM, HEAD_DIM, LANES)
    idx_tab_p = _rope_tables(pos_p, IDX_ROPE_DIM, IDX_DIM, LANES) + _rope_tables(pos_p, IDX_ROPE_DIM, IDX_DIM, IDX_DIM)
    idx_tab_s = _rope_tables(pos_s, IDX_ROPE_DIM, IDX_DIM, LANES) + _rope_tables(pos_s, IDX_ROPE_DIM, IDX_DIM, IDX_DIM)

    cache_k2 = cache_k.reshape(n_phys, depth, page * N_HEADS, HEAD_DIM)
    cache_v2 = cache_v.reshape(n_phys, depth, page * N_HEADS, HEAD_DIM)
    pt_flat = page_table.reshape(-1).astype(jnp.int32)

    w_in16 = w_in.astype(bf16)
    ffn_w = (ffn_w1.astype(bf16), ffn_w3.astype(bf16), ffn_w2.astype(bf16))
    xp = x_prompt.reshape(m_p, d)
    xs = jnp.zeros((SAMPLE_ROWS, d), f32).at[:bd].set(x_sample.reshape(bd, d))
    outs = [[] for _ in range(6)]
    k_all = v_all = None
    cache_dims = (batch, depth, t)
    for l in range(depth):
        lw = _layer_weights(w_in16, w_out[l], l, d)
        qv = _head_vectors(q_norm_g[l])
        kv = _head_vectors(k_norm_g[l])
        mod_l = mod_all[l].reshape(-1, N_MOD, d)
        mod_p = [mod_l[:batch, j].reshape(batch, 1, d) for j in range(N_MOD)]
        mod_s = [mod_l[batch:batch + SAMPLE_ROWS, j].reshape(1, SAMPLE_ROWS, d) for j in range(N_MOD)]

        xp, h = _dense_front(xp, mod_p, ffn_w, l, norm_g[l], tm_p, t)
        q16, k16, k_all, v16, v_all, qi16, kw32, kk16 = _projections(h, w_in16, l, lw, qv, kv, head_tab_p, idx_tab_p,
                                                                     tm_p, cache_dims, k_all, v_all)
        o16 = _attn_prompt(q16, k16, v16, qi16, kk16, kw32, batch, t, keep_p)
        xp = _dense_back(xp, h, o16, mod_p, lw, ffn_w, l, norm_g[l], tm_p, t)
        outs[2].append(kw32[:, :IDX_DIM].reshape(batch, t, IDX_DIM))

        xs, h = _dense_front(xs, mod_s, ffn_w, l, norm_g[l], SAMPLE_ROWS, SAMPLE_ROWS)
        q16, k16, k32, v16, v32, qi16, kw32, kk16 = _projections(h, w_in16, l, lw, qv, kv, head_tab_s, idx_tab_s,
                                                                 SAMPLE_ROWS)
        q_h = q16[:bd].reshape(bd, N_HEADS, HEAD_DIM)
        k_new = k32[:bd].reshape(bd, N_HEADS, HEAD_DIM)
        v_new = v32[:bd].reshape(bd, N_HEADS, HEAD_DIM)
        ki_new = kw32[:bd, :IDX_DIM]
        wi = jnp.broadcast_to(kw32[:bd, IDX_DIM:IDX_DIM + N_IDX_HEADS, None], (bd, N_IDX_HEADS, LANES))
        mask, selfsel = _samp_idx(pt_flat, qi16[:bd].reshape(bd, N_IDX_HEADS, IDX_DIM), wi,
                                  ki_new.reshape(bd, 1, IDX_DIM), cache_kidx, l, n_pages, keep_s)
        o_s = _samp_attn(pt_flat, q_h, k_new, v_new, mask, selfsel, cache_k2, cache_v2, l)
        o16 = jnp.zeros((SAMPLE_ROWS, MIX_W), bf16).at[:bd].set(o_s.reshape(bd, MIX_W).astype(bf16))
        xs = _dense_back(xs, h, o16, mod_s, lw, ffn_w, l, norm_g[l], SAMPLE_ROWS, SAMPLE_ROWS)
        outs[3].append(k_new.reshape(bd, 1, N_HEADS, HEAD_DIM))
        outs[4].append(v_new.reshape(bd, 1, N_HEADS, HEAD_DIM))
        outs[5].append(ki_new.reshape(bd, 1, IDX_DIM))

    stacked = [jnp.stack(o, axis=1) for o in outs[2:]]
    kv_shape = (batch, depth, t, N_HEADS, HEAD_DIM)
    return (xp.reshape(batch, t, d), xs[:bd].reshape(bd, 1, d), k_all.reshape(kv_shape), v_all.reshape(kv_shape),
            *stacked)
```
